```python
import jax, jax.numpy as jnp
from jax import lax
import numpy as np

D_MODEL = 1024
BATCH = 2
SEQ = 16384
DEPTH = 4

D_CONV = 256
CONV_KERNEL = 31
NSA_HEADS = 8
NSA_KV_HEADS = 2
HEAD_DIM = 64
GQ = NSA_HEADS // NSA_KV_HEADS
D_NSA = NSA_HEADS * HEAD_DIM
RWKV_HEADS = 4
RWKV_HEAD_DIM = 64
D_RWKV = RWKV_HEADS * RWKV_HEAD_DIM
D_MIX = D_CONV + D_NSA + D_RWKV

CMP_BLOCK = 32
CMP_STRIDE = 16
SEL_BLOCK = 64
TOP_N = 16
WINDOW = 512
Q_BLOCK = 128
FORCE_BONUS = 1.0e4

DECAY_LORA = 32
AAA_LORA = 32
GATE_LORA = 64
RWKV_GN_EPS = 64e-5

D_FF = 2816
FFN_CONV = 3

RMS_EPS = 1e-6
LN_EPS = 1e-5
NEG_INF = -1e30
TINY = 1e-30

CONV_COLS = 2 * D_CONV
KV_COLS = NSA_KV_HEADS * HEAD_DIM
NSA_COLS = D_NSA + 6 * KV_COLS + 3 * NSA_HEADS
RWKV_COLS = 3 * D_RWKV + DECAY_LORA + AAA_LORA + GATE_LORA
IN_COLS = CONV_COLS + NSA_COLS + RWKV_COLS

kernel_name = "hybrid_conv_nsa_rwkv7_trunk"


def _offsets(sizes):
    out, acc = [], 0
    for s in sizes[:-1]:
        acc += s
        out.append(acc)
    return out


def rms_norm(x, g):
    xf = x.astype(jnp.float32)
    y = xf * lax.rsqrt(jnp.mean(xf * xf, axis=-1, keepdims=True) + RMS_EPS)
    return (y * g.astype(jnp.float32)).astype(x.dtype)


def layer_norm(x, g, b, eps):
    xf = x.astype(jnp.float32)
    mu = jnp.mean(xf, axis=-1, keepdims=True)
    var = jnp.mean(jnp.square(xf - mu), axis=-1, keepdims=True)
    y = (xf - mu) * lax.rsqrt(var + eps)
    return (y * g.astype(jnp.float32) + b.astype(jnp.float32)).astype(x.dtype)


def causal_dwconv(x, w, b):
    k = w.shape[0]
    y = lax.conv_general_dilated(
        x, w[:, None, :].astype(x.dtype), window_strides=(1,),
        padding=[(k - 1, 0)], dimension_numbers=('NWC', 'WIO', 'NWC'),
        feature_group_count=x.shape[-1])
    return y + b


def token_shift(p, mu):
    prev = jnp.pad(p, ((0, 0), (1, 0), (0, 0)))[:, :-1]
    return p + (prev - p) * mu


def masked_softmax(s, mask, axes):
    s = jnp.where(mask, s, NEG_INF)
    m = jnp.max(s, axis=axes, keepdims=True)
    e = jnp.where(mask, jnp.exp(s - m), 0.0)
    return e / jnp.maximum(jnp.sum(e, axis=axes, keepdims=True), TINY)


def conv_mixer(p, dw_w, dw_b, ln_g, ln_b):
    u, gate = jnp.split(p, 2, axis=-1)
    y = u * jax.nn.sigmoid(gate)
    y = causal_dwconv(y, dw_w, dw_b)
    y = layer_norm(y, ln_g, ln_b, LN_EPS)
    return jax.nn.silu(y)


def compress_blocks(kv, pe, w1, w2):
    b, t, g, d = kv.shape
    per = CMP_BLOCK // CMP_STRIDE
    nch = t // CMP_STRIDE
    nc = nch - per + 1
    ch = kv.reshape(b, nch, CMP_STRIDE, g, d)
    blocks = jnp.concatenate([ch[:, o:o + nc] for o in range(per)], axis=2)
    blocks = blocks + pe[:, None, :]
    flat = jnp.swapaxes(blocks, 2, 3).reshape(b, nc, g, CMP_BLOCK * d)
    return jax.nn.gelu(flat @ w1) @ w2


def nsa_mixer(p, pe_k, pe_v, w1_k, w2_k, w1_v, w2_v):
    b, t, _ = p.shape
    f32 = jnp.float32
    sizes = [D_NSA] + [KV_COLS] * 6 + [3 * NSA_HEADS]
    q, kc, vc, ks, vs, kw, vw, gt = jnp.split(p, _offsets(sizes), axis=-1)
    kv_shape = (b, t, NSA_KV_HEADS, HEAD_DIM)
    q = q.reshape(b, t, NSA_KV_HEADS, GQ, HEAD_DIM)
    k_cmp = compress_blocks(kc.reshape(kv_shape), pe_k, w1_k, w2_k)
    v_cmp = compress_blocks(vc.reshape(kv_shape), pe_v, w1_v, w2_v)
    nc = k_cmp.shape[1]
    ns = t // SEL_BLOCK
    n_top = min(TOP_N, ns)
    k_blk = jnp.transpose(ks.reshape(b, ns, SEL_BLOCK, NSA_KV_HEADS, HEAD_DIM), (0, 3, 1, 2, 4))
    v_blk = jnp.transpose(vs.reshape(b, ns, SEL_BLOCK, NSA_KV_HEADS, HEAD_DIM), (0, 3, 1, 2, 4))
    k_pad = jnp.pad(kw.reshape(kv_shape), ((0, 0), (WINDOW, 0), (0, 0), (0, 0)))
    v_pad = jnp.pad(vw.reshape(kv_shape), ((0, 0), (WINDOW, 0), (0, 0), (0, 0)))
    gates = jax.nn.sigmoid(gt.reshape(b, t, NSA_KV_HEADS, GQ, 3))
    cmp_end = jnp.arange(nc) * CMP_STRIDE + (CMP_BLOCK - 1)
    blk_ids = jnp.arange(ns)
    per = CMP_BLOCK // CMP_STRIDE
    ratio = SEL_BLOCK // CMP_STRIDE
    n_off = ratio + per - 1
    scale = HEAD_DIM ** -0.5
    gather = jax.vmap(jax.vmap(lambda blk, ix: blk[ix]))

    def one_block(c):
        t0 = c * Q_BLOCK
        tq = t0 + jnp.arange(Q_BLOCK)
        qc = lax.dynamic_slice_in_dim(q, t0, Q_BLOCK, axis=1)
        s = jnp.einsum('bqgrd,bngd->bgrqn', qc, k_cmp).astype(f32) * scale
        pc = masked_softmax(s, cmp_end[None, :] <= tq[:, None], (-1,))
        o_cmp = jnp.einsum('bgrqn,bngd->bqgrd', pc.astype(v_cmp.dtype), v_cmp)
        imp = jnp.pad(jnp.sum(pc, axis=2), ((0, 0), (0, 0), (0, 0), (per - 1, n_off)))
        sel = sum(imp[..., o:o + ratio * ns:ratio] for o in range(n_off))
        cur = tq // SEL_BLOCK
        future = blk_ids[None, :] > cur[:, None]
        forced = (blk_ids[None, :] == 0) | (blk_ids[None, :] == cur[:, None]) | (blk_ids[None, :] == cur[:, None] - 1)
        score = jnp.where(future, -1.0, jnp.where(forced, sel + FORCE_BONUS, sel))
        _, idx = lax.top_k(score, n_top)
        kg = gather(k_blk, idx)
        vg = gather(v_blk, idx)
        kpos = idx[..., None] * SEL_BLOCK + jnp.arange(SEL_BLOCK)
        msel = (kpos <= tq[:, None, None])[:, :, None]
        s = jnp.einsum('bqgrd,bgqnld->bgrqnl', qc, kg).astype(f32) * scale
        ps = masked_softmax(s, msel, (-2, -1))
        o_sel = jnp.einsum('bgrqnl,bgqnld->bqgrd', ps.astype(vg.dtype), vg)
        kwc = lax.dynamic_slice_in_dim(k_pad, t0, Q_BLOCK + WINDOW, axis=1)
        vwc = lax.dynamic_slice_in_dim(v_pad, t0, Q_BLOCK + WINDOW, axis=1)
        wpos = t0 - WINDOW + jnp.arange(Q_BLOCK + WINDOW)
        diff = tq[:, None] - wpos[None, :]
        mwin = (diff >= 0) & (diff < WINDOW) & (wpos[None, :] >= 0)
        s = jnp.einsum('bqgrd,bkgd->bgrqk', qc, kwc).astype(f32) * scale
        pw = masked_softmax(s, mwin, (-1,))
        o_win = jnp.einsum('bgrqk,bkgd->bqgrd', pw.astype(vwc.dtype), vwc)
        gc = lax.dynamic_slice_in_dim(gates, t0, Q_BLOCK, axis=1)
        return gc[..., 0:1] * o_cmp + gc[..., 1:2] * o_sel + gc[..., 2:3] * o_win

    out = lax.map(one_block, jnp.arange(t // Q_BLOCK))
    return jnp.moveaxis(out, 0, 1).reshape(b, t, D_NSA)


def rwkv7_mixer(p, mu, w0, w2, a0, a2, g2, k_k, k_a, r_k, ln_g, ln_b):
    b, t, _ = p.shape
    f32 = jnp.float32
    hn = (RWKV_HEADS, RWKV_HEAD_DIM)
    p = token_shift(p, mu)
    sizes = [D_RWKV, D_RWKV, D_RWKV, DECAY_LORA, AAA_LORA, GATE_LORA]
    r, k, v, xw, xa, xg = jnp.split(p, _offsets(sizes), axis=-1)
    wlog = -jax.nn.softplus(-(w0 + jnp.tanh(xw) @ w2).astype(f32)) - 0.5
    decay = jnp.exp(-jnp.exp(wlog))
    a = jax.nn.sigmoid(a0 + xa @ a2)
    g = jax.nn.sigmoid(xg) @ g2
    hs = (b, t) + hn
    r, k, v, a, decay = [z.reshape(hs).astype(f32) for z in (r, k, v, a, decay)]
    kk = k * k_k.reshape(hn)
    kk = kk / jnp.maximum(jnp.sqrt(jnp.sum(kk * kk, axis=-1, keepdims=True)), 1e-12)
    k = k * (1.0 + (a - 1.0) * k_a.reshape(hn))

    def step(S, inp):
        r_t, w_t, k_t, v_t, kk_t, a_t = inp
        sa = jnp.einsum('bhvk,bhk->bhv', S, -kk_t)
        S = S * w_t[:, :, None, :] + sa[..., None] * (kk_t * a_t)[:, :, None, :] + v_t[..., None] * k_t[:, :, None, :]
        return S, jnp.einsum('bhvk,bhk->bhv', S, r_t)

    xs = tuple(jnp.moveaxis(z, 1, 0) for z in (r, decay, k, v, kk, a))
    S0 = jnp.zeros((b, RWKV_HEADS, RWKV_HEAD_DIM, RWKV_HEAD_DIM), f32)
    _, y = lax.scan(step, S0, xs)
    y = jnp.moveaxis(y, 0, 1)
    y = layer_norm(y, ln_g.reshape(hn), ln_b.reshape(hn), RWKV_GN_EPS)
    y = y + jnp.sum(r * k * r_k, axis=-1, keepdims=True) * v
    return (y.reshape(b, t, D_RWKV) * g.astype(f32)).astype(p.dtype)


def conv_ffn(h, w_gate, w_up, conv_w, conv_b, w_down):
    a = causal_dwconv(h @ w_gate, conv_w, conv_b)
    return (jax.nn.silu(a) * (h @ w_up)) @ w_down


def setup_inputs(seed: int = 0) -> dict:
    key = jax.random.key(seed)
    ks = jax.random.split(key, 32)
    L = DEPTH

    def nrm(k, shape, scale):
        return jax.random.normal(k, shape, jnp.float32) * scale

    return {
        "x": nrm(ks[0], (BATCH, SEQ, D_MODEL), 1.0),
        "w_in": nrm(ks[1], (L, D_MODEL, IN_COLS), D_MODEL ** -0.5),
        "w_out": nrm(ks[2], (L, D_MIX, D_MODEL), D_MIX ** -0.5),
        "norm_mix": 1.0 + nrm(ks[3], (L, D_MODEL), 0.02),
        "norm_ffn": 1.0 + nrm(ks[4], (L, D_MODEL), 0.02),
        "norm_final": 1.0 + nrm(ks[5], (D_MODEL,), 0.02),
        "conv_dw_w": nrm(ks[6], (L, CONV_KERNEL, D_CONV), CONV_KERNEL ** -0.5),
        "conv_dw_b": nrm(ks[7], (L, D_CONV), 0.01),
        "conv_ln_g": 1.0 + nrm(ks[8], (L, D_CONV), 0.02),
        "conv_ln_b": nrm(ks[9], (L, D_CONV), 0.01),
        "cmp_pe_k": nrm(ks[10], (L, CMP_BLOCK, HEAD_DIM), 0.1),
        "cmp_pe_v": nrm(ks[11], (L, CMP_BLOCK, HEAD_DIM), 0.1),
        "cmp_w1_k": nrm(ks[12], (L, CMP_BLOCK * HEAD_DIM, HEAD_DIM), (CMP_BLOCK * HEAD_DIM) ** -0.5),
        "cmp_w2_k": nrm(ks[13], (L, HEAD_DIM, HEAD_DIM), HEAD_DIM ** -0.5),
        "cmp_w1_v": nrm(ks[14], (L, CMP_BLOCK * HEAD_DIM, HEAD_DIM), (CMP_BLOCK * HEAD_DIM) ** -0.5),
        "cmp_w2_v": nrm(ks[15], (L, HEAD_DIM, HEAD_DIM), HEAD_DIM ** -0.5),
        "rwkv_mu": jax.random.uniform(ks[16], (L, RWKV_COLS), jnp.float32),
        "rwkv_w0": jax.random.uniform(ks[17], (L, D_RWKV), jnp.float32, -4.0, 1.0),
        "rwkv_w2": nrm(ks[18], (L, DECAY_LORA, D_RWKV), 0.5 * DECAY_LORA ** -0.5),
        "rwkv_a0": nrm(ks[19], (L, D_RWKV), 0.1),
        "rwkv_a2": nrm(ks[20], (L, AAA_LORA, D_RWKV), AAA_LORA ** -0.5),
        "rwkv_g2": nrm(ks[21], (L, GATE_LORA, D_RWKV), GATE_LORA ** -0.5),
        "rwkv_k_k": 0.85 + nrm(ks[22], (L, D_RWKV), 0.02),
        "rwkv_k_a": 1.0 + nrm(ks[23], (L, D_RWKV), 0.02),
        "rwkv_r_k": nrm(ks[24], (L, RWKV_HEADS, RWKV_HEAD_DIM), 0.1),
        "rwkv_ln_g": 1.0 + nrm(ks[25], (L, D_RWKV), 0.02),
        "rwkv_ln_b": nrm(ks[26], (L, D_RWKV), 0.01),
        "ffn_w_gate": nrm(ks[27], (L, D_MODEL, D_FF), D_MODEL ** -0.5),
        "ffn_w_up": nrm(ks[28], (L, D_MODEL, D_FF), D_MODEL ** -0.5),
        "ffn_conv_w": nrm(ks[29], (L, FFN_CONV, D_FF), FFN_CONV ** -0.5),
        "ffn_conv_b": nrm(ks[30], (L, D_FF), 0.01),
        "ffn_w_down": nrm(ks[31], (L, D_FF, D_MODEL), D_FF ** -0.5),
    }


def reference(x, w_in, w_out, norm_mix, norm_ffn, norm_final, conv_dw_w, conv_dw_b,
              conv_ln_g, conv_ln_b, cmp_pe_k, cmp_pe_v, cmp_w1_k, cmp_w2_k, cmp_w1_v,
              cmp_w2_v, rwkv_mu, rwkv_w0, rwkv_w2, rwkv_a0, rwkv_a2, rwkv_g2, rwkv_k_k,
              rwkv_k_a, rwkv_r_k, rwkv_ln_g, rwkv_ln_b, ffn_w_gate, ffn_w_up, ffn_conv_w,
              ffn_conv_b, ffn_w_down):
    splits = _offsets([CONV_COLS, NSA_COLS, RWKV_COLS])
    for i in range(DEPTH):
        h = rms_norm(x, norm_mix[i])
        proj = h @ w_in[i]
        p_conv, p_nsa, p_rwkv = jnp.split(proj, splits, axis=-1)
        y_conv = conv_mixer(p_conv, conv_dw_w[i], conv_dw_b[i], conv_ln_g[i], conv_ln_b[i])
        y_nsa = nsa_mixer(p_nsa, cmp_pe_k[i], cmp_pe_v[i], cmp_w1_k[i], cmp_w2_k[i],
                          cmp_w1_v[i], cmp_w2_v[i])
        y_rwkv = rwkv7_mixer(p_rwkv, rwkv_mu[i], rwkv_w0[i], rwkv_w2[i], rwkv_a0[i],
                             rwkv_a2[i], rwkv_g2[i], rwkv_k_k[i], rwkv_k_a[i], rwkv_r_k[i],
                             rwkv_ln_g[i], rwkv_ln_b[i])
        mix = jnp.concatenate([y_conv, y_nsa, y_rwkv], axis=-1)
        x = x + mix @ w_out[i]
        h = rms_norm(x, norm_ffn[i])
        x = x + conv_ffn(h, ffn_w_gate[i], ffn_w_up[i], ffn_conv_w[i], ffn_conv_b[i], ffn_w_down[i])
    return rms_norm(x, norm_final)
```

```python
import functools

import jax
import jax.numpy as jnp
from jax import lax
from jax.experimental import pallas as pl
from jax.experimental.pallas import tpu as pltpu

F32 = jnp.float32
BF16 = jnp.bfloat16

D_MODEL = 1024
DEPTH = 4
D_CONV = 256
CONV_KERNEL = 31
NSA_HEADS = 8
NSA_KV_HEADS = 2
HEAD_DIM = 64
GQ = NSA_HEADS // NSA_KV_HEADS
D_NSA = NSA_HEADS * HEAD_DIM
RWKV_HEADS = 4
RWKV_HEAD_DIM = 64
D_RWKV = RWKV_HEADS * RWKV_HEAD_DIM
CMP_BLOCK = 32
CMP_STRIDE = 16
SEL_BLOCK = 64
TOP_N = 16
WINDOW = 512
Q_BLOCK = 128
FORCE_BONUS = 1.0e4
DECAY_LORA = 32
AAA_LORA = 32
GATE_LORA = 64
RWKV_GN_EPS = 64e-5
D_FF = 2816
FFN_CONV = 3
RMS_EPS = 1e-6
LN_EPS = 1e-5
NEG_INF = -1e30
TINY = 1e-30

CONV_COLS = 2 * D_CONV
KV_COLS = NSA_KV_HEADS * HEAD_DIM
NSA_COLS = D_NSA + 6 * KV_COLS + 3 * NSA_HEADS
RWKV_COLS = 3 * D_RWKV + DECAY_LORA + AAA_LORA + GATE_LORA

LANES = 128
SUBLANES = 8
VMEM_LIMIT_BYTES = 56 * 1024 * 1024

SEL_SUPER = LANES
SEL_CHUNK = 512
WIN_KEYS = WINDOW + Q_BLOCK
SCAN_SUB = 32
CONV_HALO = 32

_NT = (((1,), (1,)), ((), ()))


def _cparams(*sem):
    return pltpu.CompilerParams(dimension_semantics=sem, vmem_limit_bytes=VMEM_LIMIT_BYTES)


def _const_spec(shape):
    nd = len(shape)
    return pl.BlockSpec(shape, lambda *_: (0,) * nd, pipeline_mode=pl.Buffered(1))


def _sigmoid(x):
    return 1.0 / (1.0 + jnp.exp(-x))


def _dot(a, b):
    return jnp.dot(a, b, preferred_element_type=F32)


def _dot3(a, b_exact):
    a1 = a.astype(BF16)
    r1 = a - a1.astype(F32)
    a2 = r1.astype(BF16)
    a3 = (r1 - a2.astype(F32)).astype(BF16)
    return _dot(a1, b_exact) + _dot(a2, b_exact) + _dot(a3, b_exact)


def _rms(x, g):
    return x * lax.rsqrt(jnp.mean(x * x, axis=-1, keepdims=True) + RMS_EPS) * g


def _norm_proj_body(x_ref, g_ref, *refs):
    n = len(refs) // 2
    hb = _rms(x_ref[...], g_ref[...]).astype(BF16)
    for w_ref, o_ref in zip(refs[:n], refs[n:]):
        o_ref[...] = _dot(hb, w_ref[...])


def _norm_proj(x2d, gain, ws, tm=512):
    m, d = x2d.shape
    in_specs = [pl.BlockSpec((tm, d), lambda i: (i, 0)), _const_spec((1, d))]
    in_specs += [_const_spec(w.shape) for w in ws]
    return pl.pallas_call(
        _norm_proj_body,
        grid=(m // tm,),
        in_specs=in_specs,
        out_specs=[pl.BlockSpec((tm, w.shape[1]), lambda i: (i, 0)) for w in ws],
        out_shape=[jax.ShapeDtypeStruct((m, w.shape[1]), F32) for w in ws],
        compiler_params=_cparams("parallel"),
        name="norm_proj",
    )(x2d, gain.reshape(1, d), *ws)


def _conv_body(p_ref, h_ref, w_ref, b_ref, lg_ref, lb_ref, o_ref):
    tc = p_ref.shape[1]
    ext = jnp.concatenate([h_ref[0], p_ref[0]], axis=0)
    y = ext[:, :D_CONV] * _sigmoid(ext[:, D_CONV:])
    row = lax.broadcasted_iota(jnp.int32, (CONV_HALO + tc, 1), 0)
    before_start = jnp.logical_and(pl.program_id(1) == 0, row < CONV_HALO)
    y = jnp.where(before_start, 0.0, y)
    acc = jnp.zeros((tc, D_CONV), F32) + b_ref[...]
    for j in range(CONV_KERNEL):
        s = CONV_KERNEL - 1 - j
        ys = y if s == 0 else pltpu.roll(y, s, 0)
        acc = acc + w_ref[j:j + 1, :] * ys[CONV_HALO:, :]
    mu = jnp.mean(acc, axis=-1, keepdims=True)
    dlt = acc - mu
    var = jnp.mean(dlt * dlt, axis=-1, keepdims=True)
    z = dlt * lax.rsqrt(var + LN_EPS) * lg_ref[...] + lb_ref[...]
    o_ref[0] = z * _sigmoid(z)


def _conv_mixer(p_conv, dw_w, dw_b, ln_g, ln_b, tc=512):
    b, t, _ = p_conv.shape
    hb = tc // CONV_HALO
    wpad = jnp.zeros((CONV_HALO, D_CONV), F32).at[:CONV_KERNEL].set(dw_w)
    return pl.pallas_call(
        _conv_body,
        grid=(b, t // tc),
        in_specs=[
            pl.BlockSpec((1, tc, CONV_COLS), lambda bi, i: (bi, i, 0)),
            pl.BlockSpec((1, CONV_HALO, CONV_COLS), lambda bi, i: (bi, jnp.maximum(i * hb - 1, 0), 0)),
            _const_spec((CONV_HALO, D_CONV)),
            _const_spec((1, D_CONV)),
            _const_spec((1, D_CONV)),
            _const_spec((1, D_CONV)),
        ],
        out_specs=pl.BlockSpec((1, tc, D_CONV), lambda bi, i: (bi, i, 0)),
        out_shape=jax.ShapeDtypeStruct((b, t, D_CONV), F32),
        compiler_params=_cparams("parallel", "parallel"),
        name="conv_mixer",
    )(p_conv, p_conv, wpad, dw_b.reshape(1, -1), ln_g.reshape(1, -1), ln_b.reshape(1, -1))


def _gelu_tanh(x):
    return 0.5 * x * (1.0 + jnp.tanh(0.7978845608028654 * (x + 0.044715 * (x * x * x))))


def _compress_body(xk_ref, xv_ref, pek_ref, pev_ref, w1k_ref, w2k_ref, w1v_ref, w2v_ref, ok_ref, ov_ref):
    half = CMP_STRIDE * HEAD_DIM
    nch = xk_ref.shape[1]

    def one(x_ref, pe_ref, w1_ref, w2_ref, o_ref):
        x = x_ref[0]
        a = _dot((x + pe_ref[:, :half]).astype(BF16), w1_ref[:half, :])
        bb = _dot((x + pe_ref[:, half:]).astype(BF16), w1_ref[half:, :])
        pre = a + pltpu.roll(bb, nch - 1, 0)
        o_ref[0] = _dot(_gelu_tanh(pre).astype(BF16), w2_ref[...]).astype(BF16)

    one(xk_ref, pek_ref, w1k_ref, w2k_ref, ok_ref)
    one(xv_ref, pev_ref, w1v_ref, w2v_ref, ov_ref)


def _compress(xk, xv, pe_k, pe_v, w1_k, w2_k, w1_v, w2_v):
    bg, nch, width = xk.shape
    xspec = pl.BlockSpec((1, nch, width), lambda i: (i, 0, 0))
    ospec = pl.BlockSpec((1, nch, HEAD_DIM), lambda i: (i, 0, 0))
    oshape = jax.ShapeDtypeStruct((bg, nch, HEAD_DIM), BF16)
    return pl.pallas_call(
        _compress_body,
        grid=(bg,),
        in_specs=[xspec, xspec, _const_spec((1, 2 * width)), _const_spec((1, 2 * width)),
                  _const_spec(w1_k.shape), _const_spec(w2_k.shape),
                  _const_spec(w1_v.shape), _const_spec(w2_v.shape)],
        out_specs=[ospec, ospec],
        out_shape=[oshape, oshape],
        compiler_params=_cparams("parallel"),
        name="nsa_compress",
    )(xk, xv, pe_k.reshape(1, -1), pe_v.reshape(1, -1), w1_k, w2_k, w1_v, w2_v)


def _softmax_rows(s, mask):
    s = jnp.where(mask, s, NEG_INF)
    m = jnp.max(s, axis=-1, keepdims=True)
    e = jnp.where(mask, jnp.exp(s - m), 0.0)
    return e / jnp.maximum(jnp.sum(e, axis=-1, keepdims=True), TINY)


def _nsa_body(q_ref, gt_ref, kc_ref, vc_ref, msel_ref, kaug_ref, vs_ref, kw_ref, vw_ref,
              o_ref, qaug_ref, *, nc, ns, n_top):
    ncp = kc_ref.shape[2]
    nsp = msel_ref.shape[1]
    n_super = nsp // SEL_SUPER
    rows = GQ * Q_BLOCK
    t0 = pl.program_id(2) * Q_BLOCK

    qf = q_ref[0] * (HEAD_DIM ** -0.5)
    q4 = jnp.concatenate([qf[:, r * HEAD_DIM:(r + 1) * HEAD_DIM] for r in range(GQ)], axis=0)
    qb = q4.astype(BF16)
    tq = t0 + lax.broadcasted_iota(jnp.int32, (Q_BLOCK, 1), 0)
    tq4 = t0 + (lax.broadcasted_iota(jnp.int32, (rows, 1), 0) & (Q_BLOCK - 1))

    s = lax.dot_general(qb, kc_ref[0, 0], _NT, preferred_element_type=F32)
    n_idx = lax.broadcasted_iota(jnp.int32, (1, ncp), 1)
    cmask = jnp.logical_and(n_idx * CMP_STRIDE + (CMP_BLOCK - 1) <= tq4, n_idx < nc)
    pc = _softmax_rows(s, cmask)
    o_cmp = _dot(pc.astype(BF16), vc_ref[0, 0])

    imp = pc[0:Q_BLOCK]
    for r in range(1, GQ):
        imp = imp + pc[r * Q_BLOCK:(r + 1) * Q_BLOCK]
    sel = _dot3(imp, msel_ref[...])
    blk = lax.broadcasted_iota(jnp.int32, (1, nsp), 1)
    blk_f = blk.astype(F32)
    cur = jnp.right_shift(tq, 6)
    future = blk > cur
    forced = jnp.logical_or(blk == 0, jnp.logical_or(blk == cur, blk == cur - 1))
    work = jnp.where(future, -1.0, jnp.where(forced, sel + FORCE_BONUS, sel))
    picked = jnp.zeros((Q_BLOCK, nsp), F32)
    for _ in range(n_top):
        mx = jnp.max(work, axis=-1, keepdims=True)
        first = jnp.min(jnp.where(work == mx, blk_f, float(nsp)), axis=-1, keepdims=True)
        hit = blk_f == first
        picked = jnp.where(hit, 1.0, picked)
        work = jnp.where(hit, -3.0, work)
    keep = jnp.logical_and(picked > 0.5, jnp.logical_not(future))
    bias = jnp.where(keep, 0.0, NEG_INF)
    bias4 = jnp.concatenate([bias] * GQ, axis=0).astype(BF16)

    for h in range(n_super):
        qaug_ref[h, :, 0:HEAD_DIM] = qb
        qaug_ref[h, :, HEAD_DIM:LANES] = jnp.zeros((rows, LANES - HEAD_DIM), BF16)
        qaug_ref[h, :, LANES:2 * LANES] = bias4[:, h * SEL_SUPER:(h + 1) * SEL_SUPER]

    chunks_per_super = SEL_SUPER * SEL_BLOCK // SEL_CHUNK

    def flash_step(j, carry, causal):
        m, l, acc = carry
        start = pl.multiple_of(j * SEL_CHUNK, SEL_CHUNK)
        qa = qaug_ref[j // chunks_per_super]
        sc = lax.dot_general(qa, kaug_ref[0, 0, pl.ds(start, SEL_CHUNK), :], _NT,
                             preferred_element_type=F32)
        if causal:
            kpos = start + lax.broadcasted_iota(jnp.int32, (1, SEL_CHUNK), 1)
            sc = jnp.where(kpos <= tq4, sc, NEG_INF)
        m_new = jnp.maximum(m, jnp.max(sc, axis=-1, keepdims=True))
        alpha = jnp.exp(m - m_new)
        p = jnp.exp(sc - m_new)
        l = alpha * l + jnp.sum(p, axis=-1, keepdims=True)
        acc = alpha * acc + _dot(p.astype(BF16), vs_ref[0, 0, pl.ds(start, SEL_CHUNK), :])
        return m_new, l, acc

    j_diag = t0 // SEL_CHUNK
    carry = (jnp.full((rows, 1), NEG_INF, F32), jnp.zeros((rows, 1), F32), jnp.zeros((rows, HEAD_DIM), F32))
    carry = lax.fori_loop(0, j_diag, functools.partial(flash_step, causal=False), carry)
    _, l_sel, acc_sel = flash_step(j_diag, carry, causal=True)
    o_sel = acc_sel / jnp.maximum(l_sel, TINY)

    ws = pl.multiple_of(jnp.maximum(t0 - WINDOW, 0), Q_BLOCK)
    sw = lax.dot_general(qb, kw_ref[0, 0, pl.ds(ws, WIN_KEYS), :], _NT, preferred_element_type=F32)
    diff = tq4 - (ws + lax.broadcasted_iota(jnp.int32, (1, WIN_KEYS), 1))
    pw = _softmax_rows(sw, jnp.logical_and(diff >= 0, diff < WINDOW))
    o_win = _dot(pw.astype(BF16), vw_ref[0, 0, pl.ds(ws, WIN_KEYS), :])

    gts = _sigmoid(gt_ref[0])
    outs = []
    for r in range(GQ):
        sl = slice(r * Q_BLOCK, (r + 1) * Q_BLOCK)
        outs.append(gts[:, 3 * r:3 * r + 1] * o_cmp[sl]
                    + gts[:, 3 * r + 1:3 * r + 2] * o_sel[sl]
                    + gts[:, 3 * r + 2:3 * r + 3] * o_win[sl])
    o_ref[0] = jnp.concatenate(outs, axis=1)


def _nsa_attention(q, gt, k_cmp, v_cmp, msel, k_aug, v_sel, k_win, v_win):
    b, t, _ = q.shape
    g = NSA_KV_HEADS
    ncp = k_cmp.shape[2]
    nsp = msel.shape[1]
    nc = t // CMP_STRIDE - CMP_BLOCK // CMP_STRIDE + 1
    ns = t // SEL_BLOCK
    body = functools.partial(_nsa_body, nc=nc, ns=ns, n_top=min(TOP_N, ns))

    def resident(shape):
        return pl.BlockSpec((1, 1) + shape, lambda bi, gi, c: (bi, gi, 0, 0), pipeline_mode=pl.Buffered(1))

    return pl.pallas_call(
        body,
        grid=(b, g, t // Q_BLOCK),
        in_specs=[
            pl.BlockSpec((1, Q_BLOCK, GQ * HEAD_DIM), lambda bi, gi, c: (bi, c, gi)),
            pl.BlockSpec((1, Q_BLOCK, LANES), lambda bi, gi, c: (bi, c, gi)),
            resident((ncp, HEAD_DIM)),
            resident((ncp, HEAD_DIM)),
            _const_spec(msel.shape),
            resident((t, 2 * LANES)),
            resident((t, HEAD_DIM)),
            resident((t, HEAD_DIM)),
            resident((t, HEAD_DIM)),
        ],
        out_specs=pl.BlockSpec((1, Q_BLOCK, GQ * HEAD_DIM), lambda bi, gi, c: (bi, c, gi)),
        out_shape=jax.ShapeDtypeStruct((b, t, D_NSA), F32),
        scratch_shapes=[pltpu.VMEM((nsp // SEL_SUPER, GQ * Q_BLOCK, 2 * LANES), BF16)],
        compiler_params=_cparams("arbitrary", "arbitrary", "arbitrary"),
        name="nsa_attention",
    )(q, gt, k_cmp, v_cmp, msel, k_aug, v_sel, k_win, v_win)


def _sel_overlap_matrix(ncp, nsp, nc, ns):
    i = jnp.arange(ncp)[:, None]
    j = jnp.arange(nsp)[None, :]
    ratio = SEL_BLOCK // CMP_STRIDE
    per = CMP_BLOCK // CMP_STRIDE
    hit = (i >= ratio * j - (per - 1)) & (i <= ratio * j + ratio - 1) & (i < nc) & (j < ns)
    return hit.astype(BF16)


def _nsa_mixer(q, kv, gt, pe_k, pe_v, w1_k, w2_k, w1_v, w2_v):
    b, t, _ = q.shape
    g = NSA_KV_HEADS
    nch = t // CMP_STRIDE
    nc = nch - CMP_BLOCK // CMP_STRIDE + 1
    ns = t // SEL_BLOCK
    nsp = -(-ns // SEL_SUPER) * SEL_SUPER

    def heads(x):
        return x.reshape(b, t, g, HEAD_DIM).transpose(0, 2, 1, 3)

    def chunks(x):
        return heads(x).reshape(b * g, nch, CMP_STRIDE * HEAD_DIM)

    kc, vc, ks, vs, kw, vw = [kv[..., i * KV_COLS:(i + 1) * KV_COLS] for i in range(6)]
    k_cmp, v_cmp = _compress(chunks(kc), chunks(vc), pe_k, pe_v,
                             w1_k.astype(BF16), w2_k.astype(BF16), w1_v.astype(BF16), w2_v.astype(BF16))
    k_cmp = k_cmp.reshape(b, g, nch, HEAD_DIM)
    v_cmp = v_cmp.reshape(b, g, nch, HEAD_DIM)

    onehot = (jnp.arange(t)[:, None] // SEL_BLOCK) % SEL_SUPER == jnp.arange(SEL_SUPER)[None, :]
    k_aug = jnp.concatenate([
        heads(ks).astype(BF16),
        jnp.zeros((b, g, t, LANES - HEAD_DIM), BF16),
        jnp.broadcast_to(onehot.astype(BF16), (b, g, t, SEL_SUPER)),
    ], axis=-1)
    msel = _sel_overlap_matrix(nch, nsp, nc, ns)
    return _nsa_attention(q, gt, k_cmp, v_cmp, msel, k_aug, heads(vs).astype(BF16),
                          heads(kw).astype(BF16), heads(vw).astype(BF16))


def _rwkv_prep_body(p_ref, ph_ref, lo_ref, loh_ref, mu_ref, mul_ref, w0_ref, w2_ref, a0_ref, a2_ref,
                    g2_ref, kk_ref, ka_ref, rk_ref, ones_ref,
                    r_o, w_o, k_o, nkk_o, kka_o, v_o, bv_o, g_o, *, tiles_per_seq):
    tm = p_ref.shape[0]
    first = (pl.program_id(0) % tiles_per_seq) == 0
    row = lax.broadcasted_iota(jnp.int32, (tm, 1), 0)

    def shifted(x_ref, h_ref, m_ref):
        x = x_ref[...]
        last = jnp.where(first, 0.0, h_ref[SUBLANES - 1:SUBLANES, :])
        prev = jnp.where(row == 0, last, pltpu.roll(x, 1, 0))
        return x + (prev - x) * m_ref[...]

    ps = shifted(p_ref, ph_ref, mu_ref)
    lo = shifted(lo_ref, loh_ref, mul_ref)
    r = ps[:, 0:D_RWKV]
    k = ps[:, D_RWKV:2 * D_RWKV]
    v = ps[:, 2 * D_RWKV:3 * D_RWKV]

    z = w0_ref[...] + _dot(jnp.tanh(lo).astype(BF16), w2_ref[...])
    softplus_neg = jnp.maximum(-z, 0.0) + jnp.log(1.0 + jnp.exp(-jnp.abs(z)))
    decay = jnp.exp(-jnp.exp(-softplus_neg - 0.5))
    a = _sigmoid(a0_ref[...] + _dot(lo.astype(BF16), a2_ref[...]))
    gate = _dot(_sigmoid(lo).astype(BF16), g2_ref[...])

    kk = k * kk_ref[...]
    kk = kk / jnp.maximum(jnp.sqrt(_dot3(kk * kk, ones_ref[...])), 1e-12)
    k2 = k * (1.0 + (a - 1.0) * ka_ref[...])
    bonus = _dot3(r * k2 * rk_ref[...], ones_ref[...])

    r_o[...] = r
    w_o[...] = decay
    k_o[...] = k2
    nkk_o[...] = -kk
    kka_o[...] = kk * a
    v_o[...] = v
    bv_o[...] = bonus * v
    g_o[...] = gate


def _rwkv_prep(p_rkv, p_lora, seq, mu, w0, w2p, a0, a2p, g2p, k_k, k_a, r_k, tm=256):
    m = p_rkv.shape[0]
    hb = tm // SUBLANES
    head_ones = (jnp.arange(D_RWKV)[:, None] // RWKV_HEAD_DIM == jnp.arange(D_RWKV)[None, :] // RWKV_HEAD_DIM)
    row_spec = lambda w: pl.BlockSpec((tm, w), lambda i: (i, 0))
    halo_spec = lambda w: pl.BlockSpec((SUBLANES, w), lambda i: (jnp.maximum(i * hb - 1, 0), 0))
    vec = lambda x: x.reshape(1, -1)
    n_lora = p_lora.shape[1]
    oshape = jax.ShapeDtypeStruct((m, D_RWKV), F32)
    return pl.pallas_call(
        functools.partial(_rwkv_prep_body, tiles_per_seq=seq // tm),
        grid=(m // tm,),
        in_specs=[row_spec(3 * D_RWKV), halo_spec(3 * D_RWKV), row_spec(n_lora), halo_spec(n_lora),
                  _const_spec((1, 3 * D_RWKV)), _const_spec((1, n_lora)),
                  _const_spec((1, D_RWKV)), _const_spec((n_lora, D_RWKV)),
                  _const_spec((1, D_RWKV)), _const_spec((n_lora, D_RWKV)),
                  _const_spec((n_lora, D_RWKV)),
                  _const_spec((1, D_RWKV)), _const_spec((1, D_RWKV)), _const_spec((1, D_RWKV)),
                  _const_spec((D_RWKV, D_RWKV))],
        out_specs=[row_spec(D_RWKV)] * 8,
        out_shape=[oshape] * 8,
        compiler_params=_cparams("parallel"),
        name="rwkv_prep",
    )(p_rkv, p_rkv, p_lora, p_lora, vec(mu[:3 * D_RWKV]), vec(mu[3 * D_RWKV:]), vec(w0), w2p,
      vec(a0), a2p, g2p, vec(k_k), vec(k_a), vec(r_k), head_ones.astype(BF16))


def _rwkv_scan_body(r_ref, w_ref, k_ref, nkk_ref, kka_ref, vt_ref, yt_ref, s_ref):
    nb = r_ref.shape[0]
    n_sub = vt_ref.shape[1]
    pairs = [(b, hp) for b in range(nb) for hp in range(D_RWKV // LANES)]
    lo_half = lax.broadcasted_iota(jnp.int32, (RWKV_HEAD_DIM, LANES), 1) < RWKV_HEAD_DIM

    @pl.when(pl.program_id(0) == 0)
    def _():
        s_ref[...] = jnp.zeros_like(s_ref)

    def half_sums(x):
        sa = jnp.sum(jnp.where(lo_half, x, 0.0), axis=1, keepdims=True)
        sb = jnp.sum(jnp.where(lo_half, 0.0, x), axis=1, keepdims=True)
        return sa, sb

    def sub_chunk(sc, carry):
        base = pl.multiple_of(sc * SCAN_SUB, SCAN_SUB)
        states = [s_ref[i] for i in range(len(pairs))]
        for tt in range(SCAN_SUB):
            for i, (b, hp) in enumerate(pairs):
                lanes = pl.ds(hp * LANES, LANES)
                tile = pl.ds(pl.multiple_of(base + (tt // SUBLANES) * SUBLANES, SUBLANES), SUBLANES)
                sub = slice(tt % SUBLANES, tt % SUBLANES + 1)

                def row(ref):
                    return ref[b, tile, lanes][sub, :]

                va = vt_ref[b, sc, pl.ds(hp * LANES, RWKV_HEAD_DIM), tt:tt + 1]
                vb = vt_ref[b, sc, pl.ds(hp * LANES + RWKV_HEAD_DIM, RWKV_HEAD_DIM), tt:tt + 1]
                s = states[i]
                sa, sb = half_sums(s * row(nkk_ref))
                s = (s * row(w_ref)
                     + jnp.where(lo_half, sa, sb) * row(kka_ref)
                     + jnp.where(lo_half, va, vb) * row(k_ref))
                states[i] = s
                ya, yb = half_sums(s * row(r_ref))
                yt_ref[b, sc, pl.ds(hp * LANES, RWKV_HEAD_DIM), tt:tt + 1] = ya
                yt_ref[b, sc, pl.ds(hp * LANES + RWKV_HEAD_DIM, RWKV_HEAD_DIM), tt:tt + 1] = yb
        for i in range(len(pairs)):
            s_ref[i] = states[i]
        return carry

    lax.fori_loop(0, n_sub, sub_chunk, 0)


def _rwkv_scan(r, w, k, nkk, kka, v, ct=256):
    b, t, _ = r.shape
    n_sub = ct // SCAN_SUB
    vt = v.reshape(b, t // SCAN_SUB, SCAN_SUB, D_RWKV).transpose(0, 1, 3, 2)
    row_spec = pl.BlockSpec((b, ct, D_RWKV), lambda i: (0, i, 0))
    t_spec = pl.BlockSpec((b, n_sub, D_RWKV, SCAN_SUB), lambda i: (0, i, 0, 0))
    yt = pl.pallas_call(
        _rwkv_scan_body,
        grid=(t // ct,),
        in_specs=[row_spec] * 5 + [t_spec],
        out_specs=t_spec,
        out_shape=jax.ShapeDtypeStruct(vt.shape, F32),
        scratch_shapes=[pltpu.VMEM((b * (D_RWKV // LANES), RWKV_HEAD_DIM, LANES), F32)],
        compiler_params=_cparams("arbitrary"),
        name="rwkv_scan",
    )(r, w, k, nkk, kka, vt)
    return yt.transpose(0, 1, 3, 2).reshape(b, t, D_RWKV)


def _out_proj_body(x_ref, yc_ref, yn_ref, yr_ref, bv_ref, g_ref, lg_ref, lb_ref, avg_ref,
                   wc_ref, wn_ref, wr_ref, o_ref):
    y = yr_ref[...]
    mu = _dot3(y, avg_ref[...])
    dlt = y - mu
    var = _dot3(dlt * dlt, avg_ref[...])
    yr = (dlt * lax.rsqrt(var + RWKV_GN_EPS) * lg_ref[...] + lb_ref[...] + bv_ref[...]) * g_ref[...]
    o_ref[...] = (x_ref[...]
                  + _dot(yc_ref[...].astype(BF16), wc_ref[...])
                  + _dot(yn_ref[...].astype(BF16), wn_ref[...])
                  + _dot(yr.astype(BF16), wr_ref[...]))


def _out_proj(x2d, y_conv, y_nsa, y_scan, bv, gate, ln_g, ln_b, wc, wn, wr, tm=512):
    m, d = x2d.shape
    head_avg = (jnp.arange(D_RWKV)[:, None] // RWKV_HEAD_DIM == jnp.arange(D_RWKV)[None, :] // RWKV_HEAD_DIM)
    head_avg = (head_avg.astype(F32) / RWKV_HEAD_DIM).astype(BF16)
    row_spec = lambda w: pl.BlockSpec((tm, w), lambda i: (i, 0))
    return pl.pallas_call(
        _out_proj_body,
        grid=(m // tm,),
        in_specs=[row_spec(d), row_spec(D_CONV), row_spec(D_NSA), row_spec(D_RWKV), row_spec(D_RWKV),
                  row_spec(D_RWKV), _const_spec((1, D_RWKV)), _const_spec((1, D_RWKV)),
                  _const_spec((D_RWKV, D_RWKV)), _const_spec(wc.shape), _const_spec(wn.shape),
                  _const_spec(wr.shape)],
        out_specs=row_spec(d),
        out_shape=jax.ShapeDtypeStruct((m, d), F32),
        compiler_params=_cparams("parallel"),
        name="out_proj",
    )(x2d, y_conv, y_nsa, y_scan, bv, gate, ln_g.reshape(1, -1), ln_b.reshape(1, -1), head_avg, wc, wn, wr)


def _ffn_body(x_ref, xh_ref, g_ref, wg_ref, wu_ref, cw_ref, cb_ref, wd_ref, o_ref, *, tiles_per_seq):
    tm = x_ref.shape[0]
    x = x_ref[...]
    hb = _rms(x, g_ref[...]).astype(BF16)
    hh = _rms(xh_ref[...], g_ref[...]).astype(BF16)
    a = _dot(hb, wg_ref[...])
    first = (pl.program_id(0) % tiles_per_seq) == 0
    ah = jnp.where(first, 0.0, _dot(hh, wg_ref[...]))
    row = lax.broadcasted_iota(jnp.int32, (tm, 1), 0)
    a1 = jnp.where(row == 0, ah[SUBLANES - 1:SUBLANES], pltpu.roll(a, 1, 0))
    a2 = jnp.where(row == 0, ah[SUBLANES - 2:SUBLANES - 1],
                   jnp.where(row == 1, ah[SUBLANES - 1:SUBLANES], pltpu.roll(a, 2, 0)))
    conv = cw_ref[0:1, :] * a2 + cw_ref[1:2, :] * a1 + cw_ref[2:3, :] * a + cb_ref[...]
    act = conv * _sigmoid(conv) * _dot(hb, wu_ref[...])
    o_ref[...] = x + _dot(act.astype(BF16), wd_ref[...])


def _ffn(x2d, seq, gain, wg, wu, conv_w, conv_b, wd, tm=256):
    m, d = x2d.shape
    f = wg.shape[1]
    hb = tm // SUBLANES
    cw = jnp.zeros((SUBLANES, f), F32).at[:FFN_CONV].set(conv_w)
    return pl.pallas_call(
        functools.partial(_ffn_body, tiles_per_seq=seq // tm),
        grid=(m // tm,),
        in_specs=[pl.BlockSpec((tm, d), lambda i: (i, 0)),
                  pl.BlockSpec((SUBLANES, d), lambda i: (jnp.maximum(i * hb - 1, 0), 0)),
                  _const_spec((1, d)), _const_spec(wg.shape), _const_spec(wu.shape),
                  _const_spec((SUBLANES, f)), _const_spec((1, f)), _const_spec(wd.shape)],
        out_specs=pl.BlockSpec((tm, d), lambda i: (i, 0)),
        out_shape=jax.ShapeDtypeStruct((m, d), F32),
        compiler_params=_cparams("parallel"),
        name="conv_ffn",
    )(x2d, x2d, gain.reshape(1, d), wg, wu, cw, conv_b.reshape(1, f), wd)


def _final_norm_body(x_ref, g_ref, o_ref):
    o_ref[...] = _rms(x_ref[...], g_ref[...])


def _final_norm(x2d, gain, tm=1024):
    m, d = x2d.shape
    return pl.pallas_call(
        _final_norm_body,
        grid=(m // tm,),
        in_specs=[pl.BlockSpec((tm, d), lambda i: (i, 0)), _const_spec((1, d))],
        out_specs=pl.BlockSpec((tm, d), lambda i: (i, 0)),
        out_shape=jax.ShapeDtypeStruct((m, d), F32),
        compiler_params=_cparams("parallel"),
        name="final_norm",
    )(x2d, gain.reshape(1, d))


def _split_w_in(w):
    o = 0
    wc = w[:, o:o + CONV_COLS]; o += CONV_COLS
    wq = w[:, o:o + D_NSA]; o += D_NSA
    wkv = w[:, o:o + 6 * KV_COLS]; o += 6 * KV_COLS
    wgt = w[:, o:o + 3 * NSA_HEADS]; o += 3 * NSA_HEADS
    wrkv = w[:, o:o + 3 * D_RWKV]; o += 3 * D_RWKV
    wlora = w[:, o:]
    per_group = 3 * GQ
    wgt_pad = jnp.zeros((w.shape[0], NSA_KV_HEADS * LANES), w.dtype)
    for g in range(NSA_KV_HEADS):
        wgt_pad = wgt_pad.at[:, g * LANES:g * LANES + per_group].set(wgt[:, g * per_group:(g + 1) * per_group])
    return [z.astype(BF16) for z in (wc, wq, wkv, wgt_pad, wrkv, wlora)]


def _pad_lora(w, row0, n_rows):
    return jnp.zeros((n_rows, w.shape[1]), F32).at[row0:row0 + w.shape[0]].set(w).astype(BF16)


def kernel(x, w_in, w_out, norm_mix, norm_ffn, norm_final, conv_dw_w, conv_dw_b, conv_ln_g, conv_ln_b, cmp_pe_k, cmp_pe_v, cmp_w1_k, cmp_w2_k, cmp_w1_v, cmp_w2_v, rwkv_mu, rwkv_w0, rwkv_w2, rwkv_a0, rwkv_a2, rwkv_g2, rwkv_k_k, rwkv_k_a, rwkv_r_k, rwkv_ln_g, rwkv_ln_b, ffn_w_gate, ffn_w_up, ffn_conv_w, ffn_conv_b, ffn_w_down):
    b, t, d = x.shape
    m = b * t
    n_lora = DECAY_LORA + AAA_LORA + GATE_LORA
    x2 = x.reshape(m, d)
    for i in range(w_in.shape[0]):
        p_conv, q, kv, gt, p_rkv, p_lora = _norm_proj(x2, norm_mix[i], _split_w_in(w_in[i]))
        y_conv = _conv_mixer(p_conv.reshape(b, t, -1), conv_dw_w[i], conv_dw_b[i], conv_ln_g[i], conv_ln_b[i])
        y_nsa = _nsa_mixer(q.reshape(b, t, -1), kv.reshape(b, t, -1), gt.reshape(b, t, -1),
                           cmp_pe_k[i], cmp_pe_v[i], cmp_w1_k[i], cmp_w2_k[i], cmp_w1_v[i], cmp_w2_v[i])
        r, w, k2, nkk, kka, v, bv, gate = _rwkv_prep(
            p_rkv, p_lora, t, rwkv_mu[i], rwkv_w0[i],
            _pad_lora(rwkv_w2[i], 0, n_lora), rwkv_a0[i],
            _pad_lora(rwkv_a2[i], DECAY_LORA, n_lora),
            _pad_lora(rwkv_g2[i], DECAY_LORA + AAA_LORA, n_lora),
            rwkv_k_k[i], rwkv_k_a[i], rwkv_r_k[i].reshape(-1))
        s3 = lambda z: z.reshape(b, t, D_RWKV)
        y_scan = _rwkv_scan(s3(r), s3(w), s3(k2), s3(nkk), s3(kka), s3(v))
        wo = w_out[i].astype(BF16)
        x2 = _out_proj(x2, y_conv.reshape(m, -1), y_nsa.reshape(m, -1), y_scan.reshape(m, -1), bv, gate,
                       rwkv_ln_g[i], rwkv_ln_b[i], wo[:D_CONV], wo[D_CONV:D_CONV + D_NSA], wo[D_CONV + D_NSA:])
        x2 = _ffn(x2, t, norm_ffn[i], ffn_w_gate[i].astype(BF16), ffn_w_up[i].astype(BF16),
                  ffn_conv_w[i], ffn_conv_b[i], ffn_w_down[i].astype(BF16))
    return _final_norm(x2, norm_final).reshape(b, t, d)
```

```python
import functools

import jax
import jax.numpy as jnp
from jax import lax
from jax.experimental import pallas as pl
from jax.experimental.pallas import tpu as pltpu

F32 = jnp.float32
BF16 = jnp.bfloat16

D_MODEL = 1024
DEPTH = 4
D_CONV = 256
CONV_KERNEL = 31
NSA_HEADS = 8
NSA_KV_HEADS = 2
HEAD_DIM = 64
GQ = NSA_HEADS // NSA_KV_HEADS
D_NSA = NSA_HEADS * HEAD_DIM
RWKV_HEADS = 4
RWKV_HEAD_DIM = 64
D_RWKV = RWKV_HEADS * RWKV_HEAD_DIM
CMP_BLOCK = 32
CMP_STRIDE = 16
SEL_BLOCK = 64
TOP_N = 16
WINDOW = 512
Q_BLOCK = 128
FORCE_BONUS = 1.0e4
DECAY_LORA = 32
AAA_LORA = 32
GATE_LORA = 64
RWKV_GN_EPS = 64e-5
D_FF = 2816
FFN_CONV = 3
RMS_EPS = 1e-6
LN_EPS = 1e-5
NEG_INF = -1e30
TINY = 1e-30

CONV_COLS = 2 * D_CONV
KV_COLS = NSA_KV_HEADS * HEAD_DIM
NSA_COLS = D_NSA + 6 * KV_COLS + 3 * NSA_HEADS
RWKV_COLS = 3 * D_RWKV + DECAY_LORA + AAA_LORA + GATE_LORA

LANES = 128
SUBLANES = 8
VMEM_LIMIT_BYTES = 56 * 1024 * 1024

SEL_SUPER = LANES
SEL_CHUNK = 1024
WIN_KEYS = WINDOW + Q_BLOCK
SCAN_SUB = 32
CONV_HALO = 32

_NT = (((1,), (1,)), ((), ()))


def _cparams(*sem):
    return pltpu.CompilerParams(dimension_semantics=sem, vmem_limit_bytes=VMEM_LIMIT_BYTES)


def _const_spec(shape):
    nd = len(shape)
    return pl.BlockSpec(shape, lambda *_: (0,) * nd, pipeline_mode=pl.Buffered(1))


def _sigmoid(x):
    return 1.0 / (1.0 + jnp.exp(-x))


def _dot(a, b):
    return jnp.dot(a, b, preferred_element_type=F32)


def _dot3(a, b_exact):
    a1 = a.astype(BF16)
    r1 = a - a1.astype(F32)
    a2 = r1.astype(BF16)
    a3 = (r1 - a2.astype(F32)).astype(BF16)
    return _dot(a1, b_exact) + _dot(a2, b_exact) + _dot(a3, b_exact)


def _dot3_nt(a_exact, b):
    b1 = b.astype(BF16)
    r1 = b - b1.astype(F32)
    b2 = r1.astype(BF16)
    b3 = (r1 - b2.astype(F32)).astype(BF16)
    d = lambda u: lax.dot_general(a_exact, u, _NT, preferred_element_type=F32)
    return d(b1) + d(b2) + d(b3)


def _rms(x, g):
    return x * lax.rsqrt(jnp.mean(x * x, axis=-1, keepdims=True) + RMS_EPS) * g


def _norm_proj_body(x_ref, g_ref, *refs):
    n = len(refs) // 2
    hb = _rms(x_ref[...], g_ref[...]).astype(BF16)
    for w_ref, o_ref in zip(refs[:n], refs[n:]):
        o_ref[...] = _dot(hb, w_ref[...])


def _norm_proj(x2d, gain, ws, tm=512):
    m, d = x2d.shape
    in_specs = [pl.BlockSpec((tm, d), lambda i: (i, 0)), _const_spec((1, d))]
    in_specs += [_const_spec(w.shape) for w in ws]
    return pl.pallas_call(
        _norm_proj_body,
        grid=(m // tm,),
        in_specs=in_specs,
        out_specs=[pl.BlockSpec((tm, w.shape[1]), lambda i: (i, 0)) for w in ws],
        out_shape=[jax.ShapeDtypeStruct((m, w.shape[1]), F32) for w in ws],
        compiler_params=_cparams("parallel"),
        name="norm_proj",
    )(x2d, gain.reshape(1, d), *ws)


def _conv_body(p_ref, h_ref, w_ref, b_ref, lg_ref, lb_ref, o_ref):
    tc = p_ref.shape[1]
    ext = jnp.concatenate([h_ref[0], p_ref[0]], axis=0)
    y = ext[:, :D_CONV] * _sigmoid(ext[:, D_CONV:])
    row = lax.broadcasted_iota(jnp.int32, (CONV_HALO + tc, 1), 0)
    before_start = jnp.logical_and(pl.program_id(1) == 0, row < CONV_HALO)
    y = jnp.where(before_start, 0.0, y)
    acc = jnp.zeros((tc, D_CONV), F32) + b_ref[...]
    for j in range(CONV_KERNEL):
        s = CONV_KERNEL - 1 - j
        ys = y if s == 0 else pltpu.roll(y, s, 0)
        acc = acc + w_ref[j:j + 1, :] * ys[CONV_HALO:, :]
    mu = jnp.mean(acc, axis=-1, keepdims=True)
    dlt = acc - mu
    var = jnp.mean(dlt * dlt, axis=-1, keepdims=True)
    z = dlt * lax.rsqrt(var + LN_EPS) * lg_ref[...] + lb_ref[...]
    o_ref[0] = z * _sigmoid(z)


def _conv_mixer(p_conv, dw_w, dw_b, ln_g, ln_b, tc=512):
    b, t, _ = p_conv.shape
    hb = tc // CONV_HALO
    wpad = jnp.zeros((CONV_HALO, D_CONV), F32).at[:CONV_KERNEL].set(dw_w)
    return pl.pallas_call(
        _conv_body,
        grid=(b, t // tc),
        in_specs=[
            pl.BlockSpec((1, tc, CONV_COLS), lambda bi, i: (bi, i, 0)),
            pl.BlockSpec((1, CONV_HALO, CONV_COLS), lambda bi, i: (bi, jnp.maximum(i * hb - 1, 0), 0)),
            _const_spec((CONV_HALO, D_CONV)),
            _const_spec((1, D_CONV)),
            _const_spec((1, D_CONV)),
            _const_spec((1, D_CONV)),
        ],
        out_specs=pl.BlockSpec((1, tc, D_CONV), lambda bi, i: (bi, i, 0)),
        out_shape=jax.ShapeDtypeStruct((b, t, D_CONV), F32),
        compiler_params=_cparams("parallel", "parallel"),
        name="conv_mixer",
    )(p_conv, p_conv, wpad, dw_b.reshape(1, -1), ln_g.reshape(1, -1), ln_b.reshape(1, -1))


def _gelu_tanh(x):
    return 0.5 * x * (1.0 + jnp.tanh(0.7978845608028654 * (x + 0.044715 * (x * x * x))))


def _compress_body(xk_ref, xv_ref, pek_ref, pev_ref, w1k_ref, w2k_ref, w1v_ref, w2v_ref, ok_ref, ov_ref):
    half = CMP_STRIDE * HEAD_DIM
    nch = xk_ref.shape[1]

    def one(x_ref, pe_ref, w1_ref, w2_ref, o_ref):
        x = x_ref[0]
        a = _dot((x + pe_ref[:, :half]).astype(BF16), w1_ref[:half, :])
        bb = _dot((x + pe_ref[:, half:]).astype(BF16), w1_ref[half:, :])
        pre = a + pltpu.roll(bb, nch - 1, 0)
        o_ref[0] = _dot(_gelu_tanh(pre).astype(BF16), w2_ref[...]).astype(BF16)

    one(xk_ref, pek_ref, w1k_ref, w2k_ref, ok_ref)
    one(xv_ref, pev_ref, w1v_ref, w2v_ref, ov_ref)


def _compress(xk, xv, pe_k, pe_v, w1_k, w2_k, w1_v, w2_v):
    bg, nch, width = xk.shape
    xspec = pl.BlockSpec((1, nch, width), lambda i: (i, 0, 0))
    ospec = pl.BlockSpec((1, nch, HEAD_DIM), lambda i: (i, 0, 0))
    oshape = jax.ShapeDtypeStruct((bg, nch, HEAD_DIM), BF16)
    return pl.pallas_call(
        _compress_body,
        grid=(bg,),
        in_specs=[xspec, xspec, _const_spec((1, 2 * width)), _const_spec((1, 2 * width)),
                  _const_spec(w1_k.shape), _const_spec(w2_k.shape),
                  _const_spec(w1_v.shape), _const_spec(w2_v.shape)],
        out_specs=[ospec, ospec],
        out_shape=[oshape, oshape],
        compiler_params=_cparams("parallel"),
        name="nsa_compress",
    )(xk, xv, pe_k.reshape(1, -1), pe_v.reshape(1, -1), w1_k, w2_k, w1_v, w2_v)


def _softmax_rows(s, mask):
    s = jnp.where(mask, s, NEG_INF)
    m = jnp.max(s, axis=-1, keepdims=True)
    e = jnp.where(mask, jnp.exp(s - m), 0.0)
    return e / jnp.maximum(jnp.sum(e, axis=-1, keepdims=True), TINY)


def _nsa_body(q_ref, gt_ref, kc_ref, vc_ref, mselt_ref, kaug_ref, vs_ref, kw_ref, vw_ref,
              o_ref, qaug_ref, *, nc, ns, n_top):
    ncp = kc_ref.shape[2]
    nsp = mselt_ref.shape[0]
    n_super = nsp // SEL_SUPER
    rows = GQ * Q_BLOCK
    t0 = pl.program_id(2) * Q_BLOCK

    qf = q_ref[0] * (HEAD_DIM ** -0.5)
    q4 = jnp.concatenate([qf[:, r * HEAD_DIM:(r + 1) * HEAD_DIM] for r in range(GQ)], axis=0)
    qb = q4.astype(BF16)
    tq4 = t0 + (lax.broadcasted_iota(jnp.int32, (rows, 1), 0) & (Q_BLOCK - 1))

    s = lax.dot_general(qb, kc_ref[0, 0], _NT, preferred_element_type=F32)
    n_idx = lax.broadcasted_iota(jnp.int32, (1, ncp), 1)
    cmask = jnp.logical_and(n_idx * CMP_STRIDE + (CMP_BLOCK - 1) <= tq4, n_idx < nc)
    pc = _softmax_rows(s, cmask)
    o_cmp = _dot(pc.astype(BF16), vc_ref[0, 0])

    imp = pc[0:Q_BLOCK]
    for r in range(1, GQ):
        imp = imp + pc[r * Q_BLOCK:(r + 1) * Q_BLOCK]
    sel = _dot3_nt(mselt_ref[...], imp)
    blk = lax.broadcasted_iota(jnp.int32, (nsp, 1), 0)
    blk_f = blk.astype(F32)
    cur = jnp.right_shift(t0 + lax.broadcasted_iota(jnp.int32, (1, Q_BLOCK), 1), 6)
    future = blk > cur
    forced = jnp.logical_or(blk == 0, jnp.logical_or(blk == cur, blk == cur - 1))
    work = jnp.where(future, -1.0, jnp.where(forced, sel + FORCE_BONUS, sel))
    picked = jnp.zeros((nsp, Q_BLOCK), F32)
    for _ in range(n_top):
        mx = jnp.max(work, axis=0, keepdims=True)
        first = jnp.min(jnp.where(work == mx, blk_f, float(nsp)), axis=0, keepdims=True)
        hit = blk_f == first
        picked = jnp.where(hit, 1.0, picked)
        work = jnp.where(hit, -3.0, work)
    keep = jnp.logical_and(picked > 0.5, jnp.logical_not(future))
    bias = jnp.where(keep, 0.0, NEG_INF).T
    bias4 = jnp.concatenate([bias] * GQ, axis=0).astype(BF16)

    for h in range(n_super):
        qaug_ref[h, :, 0:HEAD_DIM] = qb
        qaug_ref[h, :, HEAD_DIM:LANES] = jnp.zeros((rows, LANES - HEAD_DIM), BF16)
        qaug_ref[h, :, LANES:2 * LANES] = bias4[:, h * SEL_SUPER:(h + 1) * SEL_SUPER]

    chunks_per_super = SEL_SUPER * SEL_BLOCK // SEL_CHUNK

    def scores(j):
        start = pl.multiple_of(j * SEL_CHUNK, SEL_CHUNK)
        return lax.dot_general(qaug_ref[j // chunks_per_super], kaug_ref[0, 0, pl.ds(start, SEL_CHUNK), :],
                               _NT, preferred_element_type=F32)

    def accumulate(j, sc, m, l, acc):
        start = pl.multiple_of(j * SEL_CHUNK, SEL_CHUNK)
        m_new = jnp.maximum(m, jnp.max(sc, axis=-1, keepdims=True))
        alpha = jnp.exp(m - m_new)
        p = jnp.exp(sc - m_new)
        l = alpha * l + jnp.sum(p, axis=-1, keepdims=True)
        acc = alpha * acc + _dot(p.astype(BF16), vs_ref[0, 0, pl.ds(start, SEL_CHUNK), :])
        return m_new, l, acc

    def flash_step(j, carry):
        return accumulate(j, scores(j), *carry)

    j_diag = t0 // SEL_CHUNK
    carry = (jnp.full((rows, 1), NEG_INF, F32), jnp.zeros((rows, 1), F32), jnp.zeros((rows, HEAD_DIM), F32))
    carry = lax.fori_loop(0, j_diag, flash_step, carry)
    kpos = j_diag * SEL_CHUNK + lax.broadcasted_iota(jnp.int32, (1, SEL_CHUNK), 1)
    sc_last = jnp.where(kpos <= tq4, scores(j_diag), NEG_INF)
    _, l_sel, acc_sel = accumulate(j_diag, sc_last, *carry)
    o_sel = acc_sel / jnp.maximum(l_sel, TINY)

    ws = pl.multiple_of(jnp.maximum(t0 - WINDOW, 0), Q_BLOCK)
    sw = lax.dot_general(qb, kw_ref[0, 0, pl.ds(ws, WIN_KEYS), :], _NT, preferred_element_type=F32)
    diff = tq4 - (ws + lax.broadcasted_iota(jnp.int32, (1, WIN_KEYS), 1))
    pw = _softmax_rows(sw, jnp.logical_and(diff >= 0, diff < WINDOW))
    o_win = _dot(pw.astype(BF16), vw_ref[0, 0, pl.ds(ws, WIN_KEYS), :])

    gts = _sigmoid(gt_ref[0])
    outs = []
    for r in range(GQ):
        sl = slice(r * Q_BLOCK, (r + 1) * Q_BLOCK)
        outs.append(gts[:, 3 * r:3 * r + 1] * o_cmp[sl]
                    + gts[:, 3 * r + 1:3 * r + 2] * o_sel[sl]
                    + gts[:, 3 * r + 2:3 * r + 3] * o_win[sl])
    o_ref[0] = jnp.concatenate(outs, axis=1)


def _nsa_attention(q, gt, k_cmp, v_cmp, mselt, k_aug, v_sel, k_win, v_win):
    b, t, _ = q.shape
    g = NSA_KV_HEADS
    ncp = k_cmp.shape[2]
    nsp = mselt.shape[0]
    nc = t // CMP_STRIDE - CMP_BLOCK // CMP_STRIDE + 1
    ns = t // SEL_BLOCK
    body = functools.partial(_nsa_body, nc=nc, ns=ns, n_top=min(TOP_N, ns))

    def resident(shape):
        return pl.BlockSpec((1, 1) + shape, lambda bi, gi, c: (bi, gi, 0, 0), pipeline_mode=pl.Buffered(1))

    return pl.pallas_call(
        body,
        grid=(b, g, t // Q_BLOCK),
        in_specs=[
            pl.BlockSpec((1, Q_BLOCK, GQ * HEAD_DIM), lambda bi, gi, c: (bi, c, gi)),
            pl.BlockSpec((1, Q_BLOCK, LANES), lambda bi, gi, c: (bi, c, gi)),
            resident((ncp, HEAD_DIM)),
            resident((ncp, HEAD_DIM)),
            _const_spec(mselt.shape),
            resident((t, 2 * LANES)),
            resident((t, HEAD_DIM)),
            resident((t, HEAD_DIM)),
            resident((t, HEAD_DIM)),
        ],
        out_specs=pl.BlockSpec((1, Q_BLOCK, GQ * HEAD_DIM), lambda bi, gi, c: (bi, c, gi)),
        out_shape=jax.ShapeDtypeStruct((b, t, D_NSA), F32),
        scratch_shapes=[pltpu.VMEM((nsp // SEL_SUPER, GQ * Q_BLOCK, 2 * LANES), BF16)],
        compiler_params=_cparams("arbitrary", "arbitrary", "arbitrary"),
        name="nsa_attention",
    )(q, gt, k_cmp, v_cmp, mselt, k_aug, v_sel, k_win, v_win)


def _sel_overlap_matrix(ncp, nsp, nc, ns):
    i = jnp.arange(ncp)[None, :]
    j = jnp.arange(nsp)[:, None]
    ratio = SEL_BLOCK // CMP_STRIDE
    per = CMP_BLOCK // CMP_STRIDE
    hit = (i >= ratio * j - (per - 1)) & (i <= ratio * j + ratio - 1) & (i < nc) & (j < ns)
    return hit.astype(BF16)


def _nsa_mixer(q, kv, gt, pe_k, pe_v, w1_k, w2_k, w1_v, w2_v):
    b, t, _ = q.shape
    g = NSA_KV_HEADS
    nch = t // CMP_STRIDE
    nc = nch - CMP_BLOCK // CMP_STRIDE + 1
    ns = t // SEL_BLOCK
    nsp = -(-ns // SEL_SUPER) * SEL_SUPER

    def heads(x):
        return x.reshape(b, t, g, HEAD_DIM).transpose(0, 2, 1, 3)

    def chunks(x):
        return heads(x).reshape(b * g, nch, CMP_STRIDE * HEAD_DIM)

    kc, vc, ks, vs, kw, vw = [kv[..., i * KV_COLS:(i + 1) * KV_COLS] for i in range(6)]
    k_cmp, v_cmp = _compress(chunks(kc), chunks(vc), pe_k, pe_v,
                             w1_k.astype(BF16), w2_k.astype(BF16), w1_v.astype(BF16), w2_v.astype(BF16))
    k_cmp = k_cmp.reshape(b, g, nch, HEAD_DIM)
    v_cmp = v_cmp.reshape(b, g, nch, HEAD_DIM)

    onehot = (jnp.arange(t)[:, None] // SEL_BLOCK) % SEL_SUPER == jnp.arange(SEL_SUPER)[None, :]
    k_aug = jnp.concatenate([
        heads(ks).astype(BF16),
        jnp.zeros((b, g, t, LANES - HEAD_DIM), BF16),
        jnp.broadcast_to(onehot.astype(BF16), (b, g, t, SEL_SUPER)),
    ], axis=-1)
    mselt = _sel_overlap_matrix(nch, nsp, nc, ns)
    return _nsa_attention(q, gt, k_cmp, v_cmp, mselt, k_aug, heads(vs).astype(BF16),
                          heads(kw).astype(BF16), heads(vw).astype(BF16))


def _rwkv_prep_body(p_ref, ph_ref, lo_ref, loh_ref, mu_ref, mul_ref, w0_ref, w2_ref, a0_ref, a2_ref,
                    g2_ref, kk_ref, ka_ref, rk_ref, ones_ref,
                    r_o, w_o, k_o, nkk_o, kka_o, v_o, bv_o, g_o, *, tiles_per_seq):
    tm = p_ref.shape[0]
    first = (pl.program_id(0) % tiles_per_seq) == 0
    row = lax.broadcasted_iota(jnp.int32, (tm, 1), 0)

    def shifted(x_ref, h_ref, m_ref):
        x = x_ref[...]
        last = jnp.where(first, 0.0, h_ref[SUBLANES - 1:SUBLANES, :])
        prev = jnp.where(row == 0, last, pltpu.roll(x, 1, 0))
        return x + (prev - x) * m_ref[...]

    ps = shifted(p_ref, ph_ref, mu_ref)
    lo = shifted(lo_ref, loh_ref, mul_ref)
    r = ps[:, 0:D_RWKV]
    k = ps[:, D_RWKV:2 * D_RWKV]
    v = ps[:, 2 * D_RWKV:3 * D_RWKV]

    z = w0_ref[...] + _dot(jnp.tanh(lo).astype(BF16), w2_ref[...])
    softplus_neg = jnp.maximum(-z, 0.0) + jnp.log(1.0 + jnp.exp(-jnp.abs(z)))
    decay = jnp.exp(-jnp.exp(-softplus_neg - 0.5))
    a = _sigmoid(a0_ref[...] + _dot(lo.astype(BF16), a2_ref[...]))
    gate = _dot(_sigmoid(lo).astype(BF16), g2_ref[...])

    kk = k * kk_ref[...]
    kk = kk / jnp.maximum(jnp.sqrt(_dot3(kk * kk, ones_ref[...])), 1e-12)
    k2 = k * (1.0 + (a - 1.0) * ka_ref[...])
    bonus = _dot3(r * k2 * rk_ref[...], ones_ref[...])

    r_o[...] = r
    w_o[...] = decay
    k_o[...] = k2
    nkk_o[...] = -kk
    kka_o[...] = kk * a
    v_o[...] = v
    bv_o[...] = bonus * v
    g_o[...] = gate


def _rwkv_prep(p_rkv, p_lora, seq, mu, w0, w2p, a0, a2p, g2p, k_k, k_a, r_k, tm=256):
    m = p_rkv.shape[0]
    hb = tm // SUBLANES
    head_ones = (jnp.arange(D_RWKV)[:, None] // RWKV_HEAD_DIM == jnp.arange(D_RWKV)[None, :] // RWKV_HEAD_DIM)
    row_spec = lambda w: pl.BlockSpec((tm, w), lambda i: (i, 0))
    halo_spec = lambda w: pl.BlockSpec((SUBLANES, w), lambda i: (jnp.maximum(i * hb - 1, 0), 0))
    vec = lambda x: x.reshape(1, -1)
    n_lora = p_lora.shape[1]
    oshape = jax.ShapeDtypeStruct((m, D_RWKV), F32)
    return pl.pallas_call(
        functools.partial(_rwkv_prep_body, tiles_per_seq=seq // tm),
        grid=(m // tm,),
        in_specs=[row_spec(3 * D_RWKV), halo_spec(3 * D_RWKV), row_spec(n_lora), halo_spec(n_lora),
                  _const_spec((1, 3 * D_RWKV)), _const_spec((1, n_lora)),
                  _const_spec((1, D_RWKV)), _const_spec((n_lora, D_RWKV)),
                  _const_spec((1, D_RWKV)), _const_spec((n_lora, D_RWKV)),
                  _const_spec((n_lora, D_RWKV)),
                  _const_spec((1, D_RWKV)), _const_spec((1, D_RWKV)), _const_spec((1, D_RWKV)),
                  _const_spec((D_RWKV, D_RWKV))],
        out_specs=[row_spec(D_RWKV)] * 8,
        out_shape=[oshape] * 8,
        compiler_params=_cparams("parallel"),
        name="rwkv_prep",
    )(p_rkv, p_rkv, p_lora, p_lora, vec(mu[:3 * D_RWKV]), vec(mu[3 * D_RWKV:]), vec(w0), w2p,
      vec(a0), a2p, g2p, vec(k_k), vec(k_a), vec(r_k), head_ones.astype(BF16))


def _rwkv_scan_body(r_ref, w_ref, k_ref, nkk_ref, kka_ref, vt_ref, ones_ref, yt_ref, s_ref):
    nb = r_ref.shape[0]
    n_sub = vt_ref.shape[1]
    hd = RWKV_HEAD_DIM
    pairs = [(b, hp) for b in range(nb) for hp in range(D_RWKV // LANES)]
    n_rows = len(pairs) * hd
    lo_half = lax.broadcasted_iota(jnp.int32, (n_rows, LANES), 1) < hd

    @pl.when(pl.program_id(0) == 0)
    def _():
        s_ref[...] = jnp.zeros_like(s_ref)

    def half_sums(x):
        sa = jnp.sum(jnp.where(lo_half, x, 0.0), axis=1, keepdims=True)
        sb = jnp.sum(jnp.where(lo_half, 0.0, x), axis=1, keepdims=True)
        return sa, sb

    def sub_chunk(sc, carry):
        base = pl.multiple_of(sc * SCAN_SUB, SCAN_SUB)
        s = s_ref[...]
        yt_ref[:, sc] = jnp.zeros((nb,) + yt_ref.shape[2:], F32)
        for tt in range(SCAN_SUB):
            tile = pl.ds(pl.multiple_of(base + (tt // SUBLANES) * SUBLANES, SUBLANES), SUBLANES)
            sub = slice(tt % SUBLANES, tt % SUBLANES + 1)

            def rows(ref):
                return jnp.concatenate(
                    [jnp.broadcast_to(ref[b, tile, pl.ds(hp * LANES, LANES)][sub, :], (hd, LANES))
                     for b, hp in pairs], axis=0)

            def v_cols(off):
                return jnp.concatenate(
                    [vt_ref[b, sc, pl.ds(hp * LANES + off, hd), tt:tt + 1] for b, hp in pairs], axis=0)

            sa, sb = half_sums(s * rows(nkk_ref))
            s = (s * rows(w_ref)
                 + jnp.where(lo_half, sa, sb) * rows(kka_ref)
                 + jnp.where(lo_half, v_cols(0), v_cols(hd)) * rows(k_ref))
            ys = _dot3(s * rows(r_ref), ones_ref[...])
            for i, (b, hp) in enumerate(pairs):
                for lane in (tt, hd + tt):
                    yt_ref[b, sc, hp, :, lane:lane + 1] = ys[i * hd:(i + 1) * hd, lane:lane + 1]
        s_ref[...] = s
        return carry

    lax.fori_loop(0, n_sub, sub_chunk, 0)


def _rwkv_scan(r, w, k, nkk, kka, v, ct=256):
    b, t, _ = r.shape
    n_sub = ct // SCAN_SUB
    vt = v.reshape(b, t // SCAN_SUB, SCAN_SUB, D_RWKV).transpose(0, 1, 3, 2)
    n_pair = D_RWKV // LANES
    hd = RWKV_HEAD_DIM
    row_spec = pl.BlockSpec((b, ct, D_RWKV), lambda i: (0, i, 0))
    t_spec = pl.BlockSpec((b, n_sub, D_RWKV, SCAN_SUB), lambda i: (0, i, 0, 0))
    head_ones = (jnp.arange(LANES)[:, None] // hd == jnp.arange(LANES)[None, :] // hd).astype(BF16)
    yt = pl.pallas_call(
        _rwkv_scan_body,
        grid=(t // ct,),
        in_specs=[row_spec] * 5 + [t_spec, _const_spec((LANES, LANES))],
        out_specs=pl.BlockSpec((b, n_sub, n_pair, hd, LANES), lambda i: (0, i, 0, 0, 0)),
        out_shape=jax.ShapeDtypeStruct((b, t // SCAN_SUB, n_pair, hd, LANES), F32),
        scratch_shapes=[pltpu.VMEM((b * n_pair * hd, LANES), F32)],
        compiler_params=_cparams("arbitrary"),
        name="rwkv_scan",
    )(r, w, k, nkk, kka, vt, head_ones)
    y = jnp.stack([yt[..., :SCAN_SUB], yt[..., hd:hd + SCAN_SUB]], axis=3)
    return y.transpose(0, 1, 5, 2, 3, 4).reshape(b, t, D_RWKV)


def _out_proj_body(x_ref, yc_ref, yn_ref, yr_ref, bv_ref, g_ref, lg_ref, lb_ref, avg_ref,
                   wc_ref, wn_ref, wr_ref, o_ref):
    y = yr_ref[...]
    mu = _dot3(y, avg_ref[...])
    dlt = y - mu
    var = _dot3(dlt * dlt, avg_ref[...])
    yr = (dlt * lax.rsqrt(var + RWKV_GN_EPS) * lg_ref[...] + lb_ref[...] + bv_ref[...]) * g_ref[...]
    o_ref[...] = (x_ref[...]
                  + _dot(yc_ref[...].astype(BF16), wc_ref[...])
                  + _dot(yn_ref[...].astype(BF16), wn_ref[...])
                  + _dot(yr.astype(BF16), wr_ref[...]))


def _out_proj(x2d, y_conv, y_nsa, y_scan, bv, gate, ln_g, ln_b, wc, wn, wr, tm=512):
    m, d = x2d.shape
    head_avg = (jnp.arange(D_RWKV)[:, None] // RWKV_HEAD_DIM == jnp.arange(D_RWKV)[None, :] // RWKV_HEAD_DIM)
    head_avg = (head_avg.astype(F32) / RWKV_HEAD_DIM).astype(BF16)
    row_spec = lambda w: pl.BlockSpec((tm, w), lambda i: (i, 0))
    return pl.pallas_call(
        _out_proj_body,
        grid=(m // tm,),
        in_specs=[row_spec(d), row_spec(D_CONV), row_spec(D_NSA), row_spec(D_RWKV), row_spec(D_RWKV),
                  row_spec(D_RWKV), _const_spec((1, D_RWKV)), _const_spec((1, D_RWKV)),
                  _const_spec((D_RWKV, D_RWKV)), _const_spec(wc.shape), _const_spec(wn.shape),
                  _const_spec(wr.shape)],
        out_specs=row_spec(d),
        out_shape=jax.ShapeDtypeStruct((m, d), F32),
        compiler_params=_cparams("parallel"),
        name="out_proj",
    )(x2d, y_conv, y_nsa, y_scan, bv, gate, ln_g.reshape(1, -1), ln_b.reshape(1, -1), head_avg, wc, wn, wr)


def _ffn_body(x_ref, xh_ref, g_ref, wg_ref, wu_ref, cw_ref, cb_ref, wd_ref, o_ref, *, tiles_per_seq):
    tm = x_ref.shape[0]
    x = x_ref[...]
    hb = _rms(x, g_ref[...]).astype(BF16)
    hh = _rms(xh_ref[...], g_ref[...]).astype(BF16)
    a = _dot(hb, wg_ref[...])
    first = (pl.program_id(0) % tiles_per_seq) == 0
    ah = jnp.where(first, 0.0, _dot(hh, wg_ref[...]))
    row = lax.broadcasted_iota(jnp.int32, (tm, 1), 0)
    a1 = jnp.where(row == 0, ah[SUBLANES - 1:SUBLANES], pltpu.roll(a, 1, 0))
    a2 = jnp.where(row == 0, ah[SUBLANES - 2:SUBLANES - 1],
                   jnp.where(row == 1, ah[SUBLANES - 1:SUBLANES], pltpu.roll(a, 2, 0)))
    conv = cw_ref[0:1, :] * a2 + cw_ref[1:2, :] * a1 + cw_ref[2:3, :] * a + cb_ref[...]
    act = conv * _sigmoid(conv) * _dot(hb, wu_ref[...])
    o_ref[...] = x + _dot(act.astype(BF16), wd_ref[...])


def _ffn(x2d, seq, gain, wg, wu, conv_w, conv_b, wd, tm=256):
    m, d = x2d.shape
    f = wg.shape[1]
    hb = tm // SUBLANES
    cw = jnp.zeros((SUBLANES, f), F32).at[:FFN_CONV].set(conv_w)
    return pl.pallas_call(
        functools.partial(_ffn_body, tiles_per_seq=seq // tm),
        grid=(m // tm,),
        in_specs=[pl.BlockSpec((tm, d), lambda i: (i, 0)),
                  pl.BlockSpec((SUBLANES, d), lambda i: (jnp.maximum(i * hb - 1, 0), 0)),
                  _const_spec((1, d)), _const_spec(wg.shape), _const_spec(wu.shape),
                  _const_spec((SUBLANES, f)), _const_spec((1, f)), _const_spec(wd.shape)],
        out_specs=pl.BlockSpec((tm, d), lambda i: (i, 0)),
        out_shape=jax.ShapeDtypeStruct((m, d), F32),
        compiler_params=_cparams("parallel"),
        name="conv_ffn",
    )(x2d, x2d, gain.reshape(1, d), wg, wu, cw, conv_b.reshape(1, f), wd)


def _final_norm_body(x_ref, g_ref, o_ref):
    o_ref[...] = _rms(x_ref[...], g_ref[...])


def _final_norm(x2d, gain, tm=1024):
    m, d = x2d.shape
    return pl.pallas_call(
        _final_norm_body,
        grid=(m // tm,),
        in_specs=[pl.BlockSpec((tm, d), lambda i: (i, 0)), _const_spec((1, d))],
        out_specs=pl.BlockSpec((tm, d), lambda i: (i, 0)),
        out_shape=jax.ShapeDtypeStruct((m, d), F32),
        compiler_params=_cparams("parallel"),
        name="final_norm",
    )(x2d, gain.reshape(1, d))


def _split_w_in(w):
    o = 0
    wc = w[:, o:o + CONV_COLS]; o += CONV_COLS
    wq = w[:, o:o + D_NSA]; o += D_NSA
    wkv = w[:, o:o + 6 * KV_COLS]; o += 6 * KV_COLS
    wgt = w[:, o:o + 3 * NSA_HEADS]; o += 3 * NSA_HEADS
    wrkv = w[:, o:o + 3 * D_RWKV]; o += 3 * D_RWKV
    wlora = w[:, o:]
    per_group = 3 * GQ
    wgt_pad = jnp.zeros((w.shape[0], NSA_KV_HEADS * LANES), w.dtype)
    for g in range(NSA_KV_HEADS):
        wgt_pad = wgt_pad.at[:, g * LANES:g * LANES + per_group].set(wgt[:, g * per_group:(g + 1) * per_group])
    return [z.astype(BF16) for z in (wc, wq, wkv, wgt_pad, wrkv, wlora)]


def _pad_lora(w, row0, n_rows):
    return jnp.zeros((n_rows, w.shape[1]), F32).at[row0:row0 + w.shape[0]].set(w).astype(BF16)


def kernel(x, w_in, w_out, norm_mix, norm_ffn, norm_final, conv_dw_w, conv_dw_b, conv_ln_g, conv_ln_b, cmp_pe_k, cmp_pe_v, cmp_w1_k, cmp_w2_k, cmp_w1_v, cmp_w2_v, rwkv_mu, rwkv_w0, rwkv_w2, rwkv_a0, rwkv_a2, rwkv_g2, rwkv_k_k, rwkv_k_a, rwkv_r_k, rwkv_ln_g, rwkv_ln_b, ffn_w_gate, ffn_w_up, ffn_conv_w, ffn_conv_b, ffn_w_down):
    b, t, d = x.shape
    m = b * t
    n_lora = DECAY_LORA + AAA_LORA + GATE_LORA
    x2 = x.reshape(m, d)
    for i in range(w_in.shape[0]):
        p_conv, q, kv, gt, p_rkv, p_lora = _norm_proj(x2, norm_mix[i], _split_w_in(w_in[i]))
        y_conv = _conv_mixer(p_conv.reshape(b, t, -1), conv_dw_w[i], conv_dw_b[i], conv_ln_g[i], conv_ln_b[i])
        y_nsa = _nsa_mixer(q.reshape(b, t, -1), kv.reshape(b, t, -1), gt.reshape(b, t, -1),
                           cmp_pe_k[i], cmp_pe_v[i], cmp_w1_k[i], cmp_w2_k[i], cmp_w1_v[i], cmp_w2_v[i])
        r, w, k2, nkk, kka, v, bv, gate = _rwkv_prep(
            p_rkv, p_lora, t, rwkv_mu[i], rwkv_w0[i],
            _pad_lora(rwkv_w2[i], 0, n_lora), rwkv_a0[i],
            _pad_lora(rwkv_a2[i], DECAY_LORA, n_lora),
            _pad_lora(rwkv_g2[i], DECAY_LORA + AAA_LORA, n_lora),
            rwkv_k_k[i], rwkv_k_a[i], rwkv_r_k[i].reshape(-1))
        s3 = lambda z: z.reshape(b, t, D_RWKV)
        y_scan = _rwkv_scan(s3(r), s3(w), s3(k2), s3(nkk), s3(kka), s3(v))
        wo = w_out[i].astype(BF16)
        x2 = _out_proj(x2, y_conv.reshape(m, -1), y_nsa.reshape(m, -1), y_scan.reshape(m, -1), bv, gate,
                       rwkv_ln_g[i], rwkv_ln_b[i], wo[:D_CONV], wo[D_CONV:D_CONV + D_NSA], wo[D_CONV + D_NSA:])
        x2 = _ffn(x2, t, norm_ffn[i], ffn_w_gate[i].astype(BF16), ffn_w_up[i].astype(BF16),
                  ffn_conv_w[i], ffn_conv_b[i], ffn_w_down[i].astype(BF16))
    return _final_norm(x2, norm_final).reshape(b, t, d)
```

```python
import functools

import jax
import jax.numpy as jnp
from jax import lax
from jax.experimental import pallas as pl
from jax.experimental.pallas import tpu as pltpu

F32 = jnp.float32
BF16 = jnp.bfloat16

D_MODEL = 1024
DEPTH = 4
D_CONV = 256
CONV_KERNEL = 31
NSA_HEADS = 8
NSA_KV_HEADS = 2
HEAD_DIM = 64
GQ = NSA_HEADS // NSA_KV_HEADS
D_NSA = NSA_HEADS * HEAD_DIM
RWKV_HEADS = 4
RWKV_HEAD_DIM = 64
D_RWKV = RWKV_HEADS * RWKV_HEAD_DIM
CMP_BLOCK = 32
CMP_STRIDE = 16
SEL_BLOCK = 64
TOP_N = 16
WINDOW = 512
Q_BLOCK = 128
FORCE_BONUS = 1.0e4
DECAY_LORA = 32
AAA_LORA = 32
GATE_LORA = 64
RWKV_GN_EPS = 64e-5
D_FF = 2816
FFN_CONV = 3
RMS_EPS = 1e-6
LN_EPS = 1e-5
NEG_INF = -1e30
TINY = 1e-30
LOG2_E = 1.4426950408889634

CONV_COLS = 2 * D_CONV
KV_COLS = NSA_KV_HEADS * HEAD_DIM
NSA_COLS = D_NSA + 6 * KV_COLS + 3 * NSA_HEADS
RWKV_COLS = 3 * D_RWKV + DECAY_LORA + AAA_LORA + GATE_LORA

LANES = 128
SUBLANES = 8
VMEM_LIMIT_BYTES = 56 * 1024 * 1024

SEL_SUPER = LANES
SEL_CHUNK = 1024
WIN_KEYS = WINDOW + Q_BLOCK
SCAN_SUB = 32
CONV_HALO = 32

_NT = (((1,), (1,)), ((), ()))


def _cparams(*sem):
    return pltpu.CompilerParams(dimension_semantics=sem, vmem_limit_bytes=VMEM_LIMIT_BYTES)


def _const_spec(shape):
    nd = len(shape)
    return pl.BlockSpec(shape, lambda *_: (0,) * nd, pipeline_mode=pl.Buffered(1))


def _sigmoid(x):
    return 1.0 / (1.0 + jnp.exp(-x))


def _dot(a, b):
    return jnp.dot(a, b, preferred_element_type=F32)


def _dot3(a, b_exact):
    a1 = a.astype(BF16)
    r1 = a - a1.astype(F32)
    a2 = r1.astype(BF16)
    a3 = (r1 - a2.astype(F32)).astype(BF16)
    return _dot(a1, b_exact) + _dot(a2, b_exact) + _dot(a3, b_exact)


def _rms(x, g):
    return x * lax.rsqrt(jnp.mean(x * x, axis=-1, keepdims=True) + RMS_EPS) * g


def _norm_proj_body(x_ref, g_ref, *refs):
    n = len(refs) // 2
    hb = _rms(x_ref[...], g_ref[...]).astype(BF16)
    for w_ref, o_ref in zip(refs[:n], refs[n:]):
        o_ref[...] = _dot(hb, w_ref[...])


def _norm_proj(x2d, gain, ws, tm=512):
    m, d = x2d.shape
    in_specs = [pl.BlockSpec((tm, d), lambda i: (i, 0)), _const_spec((1, d))]
    in_specs += [_const_spec(w.shape) for w in ws]
    return pl.pallas_call(
        _norm_proj_body,
        grid=(m // tm,),
        in_specs=in_specs,
        out_specs=[pl.BlockSpec((tm, w.shape[1]), lambda i: (i, 0)) for w in ws],
        out_shape=[jax.ShapeDtypeStruct((m, w.shape[1]), F32) for w in ws],
        compiler_params=_cparams("parallel"),
        name="norm_proj",
    )(x2d, gain.reshape(1, d), *ws)


def _conv_body(p_ref, h_ref, w_ref, b_ref, lg_ref, lb_ref, o_ref):
    tc = p_ref.shape[1]
    ext = jnp.concatenate([h_ref[0], p_ref[0]], axis=0)
    y = ext[:, :D_CONV] * _sigmoid(ext[:, D_CONV:])
    row = lax.broadcasted_iota(jnp.int32, (CONV_HALO + tc, 1), 0)
    before_start = jnp.logical_and(pl.program_id(1) == 0, row < CONV_HALO)
    y = jnp.where(before_start, 0.0, y)
    acc = jnp.zeros((tc, D_CONV), F32) + b_ref[...]
    for j in range(CONV_KERNEL):
        s = CONV_KERNEL - 1 - j
        ys = y if s == 0 else pltpu.roll(y, s, 0)
        acc = acc + w_ref[j:j + 1, :] * ys[CONV_HALO:, :]
    mu = jnp.mean(acc, axis=-1, keepdims=True)
    dlt = acc - mu
    var = jnp.mean(dlt * dlt, axis=-1, keepdims=True)
    z = dlt * lax.rsqrt(var + LN_EPS) * lg_ref[...] + lb_ref[...]
    o_ref[0] = z * _sigmoid(z)


def _conv_mixer(p_conv, dw_w, dw_b, ln_g, ln_b, tc=512):
    b, t, _ = p_conv.shape
    hb = tc // CONV_HALO
    wpad = jnp.zeros((CONV_HALO, D_CONV), F32).at[:CONV_KERNEL].set(dw_w)
    return pl.pallas_call(
        _conv_body,
        grid=(b, t // tc),
        in_specs=[
            pl.BlockSpec((1, tc, CONV_COLS), lambda bi, i: (bi, i, 0)),
            pl.BlockSpec((1, CONV_HALO, CONV_COLS), lambda bi, i: (bi, jnp.maximum(i * hb - 1, 0), 0)),
            _const_spec((CONV_HALO, D_CONV)),
            _const_spec((1, D_CONV)),
            _const_spec((1, D_CONV)),
            _const_spec((1, D_CONV)),
        ],
        out_specs=pl.BlockSpec((1, tc, D_CONV), lambda bi, i: (bi, i, 0)),
        out_shape=jax.ShapeDtypeStruct((b, t, D_CONV), F32),
        compiler_params=_cparams("parallel", "parallel"),
        name="conv_mixer",
    )(p_conv, p_conv, wpad, dw_b.reshape(1, -1), ln_g.reshape(1, -1), ln_b.reshape(1, -1))


def _gelu_tanh(x):
    return 0.5 * x * (1.0 + jnp.tanh(0.7978845608028654 * (x + 0.044715 * (x * x * x))))


def _compress_body(xk_ref, xv_ref, pek_ref, pev_ref, w1k_ref, w2k_ref, w1v_ref, w2v_ref, ok_ref, ov_ref):
    half = CMP_STRIDE * HEAD_DIM
    nch = xk_ref.shape[1]

    def one(x_ref, pe_ref, w1_ref, w2_ref, o_ref):
        x = x_ref[0]
        a = _dot((x + pe_ref[:, :half]).astype(BF16), w1_ref[:half, :])
        bb = _dot((x + pe_ref[:, half:]).astype(BF16), w1_ref[half:, :])
        pre = a + pltpu.roll(bb, nch - 1, 0)
        o_ref[0] = _dot(_gelu_tanh(pre).astype(BF16), w2_ref[...]).astype(BF16)

    one(xk_ref, pek_ref, w1k_ref, w2k_ref, ok_ref)
    one(xv_ref, pev_ref, w1v_ref, w2v_ref, ov_ref)


def _compress(xk, xv, pe_k, pe_v, w1_k, w2_k, w1_v, w2_v):
    bg, nch, width = xk.shape
    xspec = pl.BlockSpec((1, nch, width), lambda i: (i, 0, 0))
    ospec = pl.BlockSpec((1, nch, HEAD_DIM), lambda i: (i, 0, 0))
    oshape = jax.ShapeDtypeStruct((bg, nch, HEAD_DIM), BF16)
    return pl.pallas_call(
        _compress_body,
        grid=(bg,),
        in_specs=[xspec, xspec, _const_spec((1, 2 * width)), _const_spec((1, 2 * width)),
                  _const_spec(w1_k.shape), _const_spec(w2_k.shape),
                  _const_spec(w1_v.shape), _const_spec(w2_v.shape)],
        out_specs=[ospec, ospec],
        out_shape=[oshape, oshape],
        compiler_params=_cparams("parallel"),
        name="nsa_compress",
    )(xk, xv, pe_k.reshape(1, -1), pe_v.reshape(1, -1), w1_k, w2_k, w1_v, w2_v)


def _softmax_rows(s2, mask):
    s2 = jnp.where(mask, s2, NEG_INF)
    m = jnp.max(s2, axis=-1, keepdims=True)
    e = jnp.where(mask, jnp.exp2(s2 - m), 0.0)
    return e / jnp.maximum(jnp.sum(e, axis=-1, keepdims=True), TINY)


def _nsa_body(q_ref, gt_ref, kc_ref, vc_ref, kaug_ref, vs_ref, kw_ref, vw_ref,
              o_ref, qaug_ref, sbuf_ref, impt_ref, *, nc, n_top):
    ncp = kc_ref.shape[2]
    n_super = qaug_ref.shape[0]
    nsp = n_super * SEL_SUPER
    rows = GQ * Q_BLOCK
    t0 = pl.program_id(2) * Q_BLOCK

    qf = q_ref[0] * (HEAD_DIM ** -0.5 * LOG2_E)
    q4 = jnp.concatenate([qf[:, r * HEAD_DIM:(r + 1) * HEAD_DIM] for r in range(GQ)], axis=0)
    qb = q4.astype(BF16)
    tq4 = t0 + (lax.broadcasted_iota(jnp.int32, (rows, 1), 0) & (Q_BLOCK - 1))

    s = lax.dot_general(qb, kc_ref[0, 0], _NT, preferred_element_type=F32)
    n_idx = lax.broadcasted_iota(jnp.int32, (1, ncp), 1)
    cmp_end = jnp.where(n_idx < nc, n_idx * CMP_STRIDE + (CMP_BLOCK - 1), jnp.iinfo(jnp.int32).max)
    pc = _softmax_rows(s, cmp_end <= tq4)
    o_cmp = _dot(pc.astype(BF16), vc_ref[0, 0])

    ws = pl.multiple_of(jnp.maximum(t0 - WINDOW, 0), Q_BLOCK)
    sw = lax.dot_general(qb, kw_ref[0, 0, pl.ds(ws, WIN_KEYS), :], _NT, preferred_element_type=F32)
    diff = tq4 - (ws + lax.broadcasted_iota(jnp.int32, (1, WIN_KEYS), 1))
    in_window = lax.bitcast_convert_type(diff, jnp.uint32) < jnp.uint32(WINDOW)
    sw = jnp.where(in_window, sw, NEG_INF)
    ew = jnp.exp2(sw - jnp.max(sw, axis=-1, keepdims=True))
    aw = _dot(ew.astype(BF16), vw_ref[0, 0, pl.ds(ws, WIN_KEYS), :])
    o_win = aw[:, :HEAD_DIM] / jnp.maximum(aw[:, HEAD_DIM:HEAD_DIM + 1], TINY)
    gts = _sigmoid(gt_ref[0])

    imp = pc[0:Q_BLOCK]
    for r in range(1, GQ):
        imp = imp + pc[r * Q_BLOCK:(r + 1) * Q_BLOCK]
    impt_ref[0:SUBLANES, :] = jnp.zeros((SUBLANES, Q_BLOCK), F32)
    impt_ref[SUBLANES:SUBLANES + ncp, :] = imp.T
    if impt_ref.shape[0] > SUBLANES + ncp:
        impt_ref[SUBLANES + ncp:, :] = jnp.zeros((impt_ref.shape[0] - SUBLANES - ncp, Q_BLOCK), F32)
    ratio = SEL_BLOCK // CMP_STRIDE
    sel = impt_ref[pl.ds(SUBLANES - 1, nsp, stride=ratio), :]
    for o in range(ratio):
        sel = sel + impt_ref[pl.ds(SUBLANES + o, nsp, stride=ratio), :]
    blk = lax.broadcasted_iota(jnp.int32, (nsp, 1), 0)
    blk_f = blk.astype(F32)
    cur = jnp.right_shift(t0 + lax.broadcasted_iota(jnp.int32, (1, Q_BLOCK), 1), 6)
    future = blk > cur
    forced = jnp.logical_or(blk == 0, jnp.logical_or(blk == cur, blk == cur - 1))
    work = jnp.where(forced, -3.0, jnp.where(future, -1.0, sel))
    picked = jnp.where(forced, 1.0, 0.0)
    for _ in range(n_top - 3):
        mx = jnp.max(work, axis=0, keepdims=True)
        first = jnp.min(jnp.where(work == mx, blk_f, float(nsp)), axis=0, keepdims=True)
        hit = blk_f == first
        picked = jnp.where(hit, 1.0, picked)
        work = jnp.where(hit, -3.0, work)
    keep = jnp.where(future, 0.0, picked).astype(BF16)
    eye = (lax.broadcasted_iota(jnp.int32, (Q_BLOCK, Q_BLOCK), 0)
           == lax.broadcasted_iota(jnp.int32, (Q_BLOCK, Q_BLOCK), 1)).astype(BF16)
    keep_q = lax.dot_general(eye, keep, _NT, preferred_element_type=F32)
    bias = jnp.where(keep_q > 0.5, 0.0, NEG_INF)
    bias4 = jnp.concatenate([bias] * GQ, axis=0).astype(BF16)

    for h in range(n_super):
        qaug_ref[h, :, 0:HEAD_DIM] = qb
        qaug_ref[h, :, HEAD_DIM:LANES] = jnp.zeros((rows, LANES - HEAD_DIM), BF16)
        qaug_ref[h, :, LANES:2 * LANES] = bias4[:, h * SEL_SUPER:(h + 1) * SEL_SUPER]

    chunks_per_super = SEL_SUPER * SEL_BLOCK // SEL_CHUNK

    def scores(j):
        start = pl.multiple_of(j * SEL_CHUNK, SEL_CHUNK)
        return lax.dot_general(qaug_ref[j // chunks_per_super], kaug_ref[0, 0, pl.ds(start, SEL_CHUNK), :],
                               _NT, preferred_element_type=F32)

    def accumulate(j, sc, m, acc):
        start = pl.multiple_of(j * SEL_CHUNK, SEL_CHUNK)
        m_new = jnp.maximum(m, jnp.max(sc, axis=-1, keepdims=True))
        p = jnp.exp2(sc - m_new)
        acc = jnp.exp2(m - m_new) * acc + _dot(p.astype(BF16), vs_ref[0, 0, pl.ds(start, SEL_CHUNK), :])
        return m_new, acc

    def flash_pair(i, carry):
        j = 2 * i
        sbuf_ref[1] = scores(j + 1)
        carry = accumulate(j, sbuf_ref[0], *carry)
        sbuf_ref[0] = scores(j + 2)
        return accumulate(j + 1, sbuf_ref[1], *carry)

    def diagonal(j, sc, carry):
        kpos = j * SEL_CHUNK + lax.broadcasted_iota(jnp.int32, (1, SEL_CHUNK), 1)
        return accumulate(j, jnp.where(kpos <= tq4, sc, NEG_INF), *carry)

    def tail_two(carry):
        sbuf_ref[1] = scores(j_diag)
        carry = accumulate(j_diag - 1, sbuf_ref[0], *carry)
        return diagonal(j_diag, sbuf_ref[1], carry)

    def tail_one(carry):
        return diagonal(j_diag, sbuf_ref[0], carry)

    j_diag = t0 // SEL_CHUNK
    n_pairs = j_diag // 2
    carry = (jnp.full((rows, 1), NEG_INF, F32), jnp.zeros((rows, LANES), F32))
    sbuf_ref[0] = scores(0)
    carry = lax.fori_loop(0, n_pairs, flash_pair, carry)
    _, acc_sel = lax.cond(j_diag > 2 * n_pairs, tail_two, tail_one, carry)
    o_sel = acc_sel[:, :HEAD_DIM] / jnp.maximum(acc_sel[:, HEAD_DIM:HEAD_DIM + 1], TINY)

    outs = []
    for r in range(GQ):
        sl = slice(r * Q_BLOCK, (r + 1) * Q_BLOCK)
        outs.append(gts[:, 3 * r:3 * r + 1] * o_cmp[sl]
                    + gts[:, 3 * r + 1:3 * r + 2] * o_sel[sl]
                    + gts[:, 3 * r + 2:3 * r + 3] * o_win[sl])
    o_ref[0] = jnp.concatenate(outs, axis=1)


def _nsa_attention(q, gt, k_cmp, v_cmp, k_aug, v_sel, k_win, v_win):
    b, t, _ = q.shape
    g = NSA_KV_HEADS
    ncp = k_cmp.shape[2]
    nc = t // CMP_STRIDE - CMP_BLOCK // CMP_STRIDE + 1
    ns = t // SEL_BLOCK
    nsp = -(-ns // SEL_SUPER) * SEL_SUPER
    impt_rows = SUBLANES + max(ncp, nsp * (SEL_BLOCK // CMP_STRIDE))
    body = functools.partial(_nsa_body, nc=nc, n_top=min(TOP_N, ns))

    def resident(shape):
        return pl.BlockSpec((1, 1) + shape, lambda bi, gi, c: (bi, gi, 0, 0), pipeline_mode=pl.Buffered(1))

    return pl.pallas_call(
        body,
        grid=(b, g, t // Q_BLOCK),
        in_specs=[
            pl.BlockSpec((1, Q_BLOCK, GQ * HEAD_DIM), lambda bi, gi, c: (bi, c, gi)),
            pl.BlockSpec((1, Q_BLOCK, LANES), lambda bi, gi, c: (bi, c, gi)),
            resident((ncp, HEAD_DIM)),
            resident((ncp, HEAD_DIM)),
            resident((t, 2 * LANES)),
            resident((t, LANES)),
            resident((t, HEAD_DIM)),
            resident((t, LANES)),
        ],
        out_specs=pl.BlockSpec((1, Q_BLOCK, GQ * HEAD_DIM), lambda bi, gi, c: (bi, c, gi)),
        out_shape=jax.ShapeDtypeStruct((b, t, D_NSA), F32),
        scratch_shapes=[pltpu.VMEM((nsp // SEL_SUPER, GQ * Q_BLOCK, 2 * LANES), BF16),
                        pltpu.VMEM((2, GQ * Q_BLOCK, SEL_CHUNK), F32),
                        pltpu.VMEM((impt_rows, Q_BLOCK), F32)],
        compiler_params=_cparams("arbitrary", "arbitrary", "arbitrary"),
        name="nsa_attention",
    )(q, gt, k_cmp, v_cmp, k_aug, v_sel, k_win, v_win)


def _nsa_mixer(q, kv, gt, pe_k, pe_v, w1_k, w2_k, w1_v, w2_v):
    b, t, _ = q.shape
    g = NSA_KV_HEADS
    nch = t // CMP_STRIDE
    nc = nch - CMP_BLOCK // CMP_STRIDE + 1
    ns = t // SEL_BLOCK
    nsp = -(-ns // SEL_SUPER) * SEL_SUPER

    def heads(x):
        return x.reshape(b, t, g, HEAD_DIM).transpose(0, 2, 1, 3)

    def chunks(x):
        return heads(x).reshape(b * g, nch, CMP_STRIDE * HEAD_DIM)

    kc, vc, ks, vs, kw, vw = [kv[..., i * KV_COLS:(i + 1) * KV_COLS] for i in range(6)]
    k_cmp, v_cmp = _compress(chunks(kc), chunks(vc), pe_k, pe_v,
                             w1_k.astype(BF16), w2_k.astype(BF16), w1_v.astype(BF16), w2_v.astype(BF16))
    k_cmp = k_cmp.reshape(b, g, nch, HEAD_DIM)
    v_cmp = v_cmp.reshape(b, g, nch, HEAD_DIM)

    onehot = (jnp.arange(t)[:, None] // SEL_BLOCK) % SEL_SUPER == jnp.arange(SEL_SUPER)[None, :]
    k_aug = jnp.concatenate([
        heads(ks).astype(BF16),
        jnp.zeros((b, g, t, LANES - HEAD_DIM), BF16),
        jnp.broadcast_to(onehot.astype(BF16), (b, g, t, SEL_SUPER)),
    ], axis=-1)
    ones_col = (jnp.arange(LANES - HEAD_DIM) == 0).astype(BF16)

    def with_ones(x):
        return jnp.concatenate([heads(x).astype(BF16), jnp.broadcast_to(ones_col, (b, g, t, LANES - HEAD_DIM))],
                               axis=-1)

    return _nsa_attention(q, gt, k_cmp, v_cmp, k_aug, with_ones(vs),
                          heads(kw).astype(BF16), with_ones(vw))


def _rwkv_prep_body(p_ref, ph_ref, lo_ref, loh_ref, mu_ref, mul_ref, w0_ref, w2_ref, a0_ref, a2_ref,
                    g2_ref, kk_ref, ka_ref, rk_ref, ones_ref,
                    r_o, w_o, k_o, nkk_o, kka_o, v_o, bv_o, g_o, *, tiles_per_seq):
    tm = p_ref.shape[0]
    first = (pl.program_id(0) % tiles_per_seq) == 0
    row = lax.broadcasted_iota(jnp.int32, (tm, 1), 0)

    def shifted(x_ref, h_ref, m_ref):
        x = x_ref[...]
        last = jnp.where(first, 0.0, h_ref[SUBLANES - 1:SUBLANES, :])
        prev = jnp.where(row == 0, last, pltpu.roll(x, 1, 0))
        return x + (prev - x) * m_ref[...]

    ps = shifted(p_ref, ph_ref, mu_ref)
    lo = shifted(lo_ref, loh_ref, mul_ref)
    r = ps[:, 0:D_RWKV]
    k = ps[:, D_RWKV:2 * D_RWKV]
    v = ps[:, 2 * D_RWKV:3 * D_RWKV]

    z = w0_ref[...] + _dot(jnp.tanh(lo).astype(BF16), w2_ref[...])
    softplus_neg = jnp.maximum(-z, 0.0) + jnp.log(1.0 + jnp.exp(-jnp.abs(z)))
    decay = jnp.exp(-jnp.exp(-softplus_neg - 0.5))
    a = _sigmoid(a0_ref[...] + _dot(lo.astype(BF16), a2_ref[...]))
    gate = _dot(_sigmoid(lo).astype(BF16), g2_ref[...])

    kk = k * kk_ref[...]
    kk = kk / jnp.maximum(jnp.sqrt(_dot3(kk * kk, ones_ref[...])), 1e-12)
    k2 = k * (1.0 + (a - 1.0) * ka_ref[...])
    bonus = _dot3(r * k2 * rk_ref[...], ones_ref[...])

    r_o[...] = r
    w_o[...] = decay
    k_o[...] = k2
    nkk_o[...] = -kk
    kka_o[...] = kk * a
    v_o[...] = v
    bv_o[...] = bonus * v
    g_o[...] = gate


def _rwkv_prep(p_rkv, p_lora, seq, mu, w0, w2p, a0, a2p, g2p, k_k, k_a, r_k, tm=256):
    m = p_rkv.shape[0]
    hb = tm // SUBLANES
    head_ones = (jnp.arange(D_RWKV)[:, None] // RWKV_HEAD_DIM == jnp.arange(D_RWKV)[None, :] // RWKV_HEAD_DIM)
    row_spec = lambda w: pl.BlockSpec((tm, w), lambda i: (i, 0))
    halo_spec = lambda w: pl.BlockSpec((SUBLANES, w), lambda i: (jnp.maximum(i * hb - 1, 0), 0))
    vec = lambda x: x.reshape(1, -1)
    n_lora = p_lora.shape[1]
    oshape = jax.ShapeDtypeStruct((m, D_RWKV), F32)
    return pl.pallas_call(
        functools.partial(_rwkv_prep_body, tiles_per_seq=seq // tm),
        grid=(m // tm,),
        in_specs=[row_spec(3 * D_RWKV), halo_spec(3 * D_RWKV), row_spec(n_lora), halo_spec(n_lora),
                  _const_spec((1, 3 * D_RWKV)), _const_spec((1, n_lora)),
                  _const_spec((1, D_RWKV)), _const_spec((n_lora, D_RWKV)),
                  _const_spec((1, D_RWKV)), _const_spec((n_lora, D_RWKV)),
                  _const_spec((n_lora, D_RWKV)),
                  _const_spec((1, D_RWKV)), _const_spec((1, D_RWKV)), _const_spec((1, D_RWKV)),
                  _const_spec((D_RWKV, D_RWKV))],
        out_specs=[row_spec(D_RWKV)] * 8,
        out_shape=[oshape] * 8,
        compiler_params=_cparams("parallel"),
        name="rwkv_prep",
    )(p_rkv, p_rkv, p_lora, p_lora, vec(mu[:3 * D_RWKV]), vec(mu[3 * D_RWKV:]), vec(w0), w2p,
      vec(a0), a2p, g2p, vec(k_k), vec(k_a), vec(r_k), head_ones.astype(BF16))


def _rwkv_scan_body(r_ref, w_ref, k_ref, nkk_ref, kka_ref, vt_ref, ones_ref, yt_ref, s_ref):
    nb = r_ref.shape[0]
    n_sub = vt_ref.shape[1]
    hd = RWKV_HEAD_DIM
    pairs = [(b, hp) for b in range(nb) for hp in range(D_RWKV // LANES)]
    n_rows = len(pairs) * hd
    lo_half = lax.broadcasted_iota(jnp.int32, (n_rows, LANES), 1) < hd

    @pl.when(pl.program_id(0) == 0)
    def _():
        s_ref[...] = jnp.zeros_like(s_ref)

    def half_sums(x):
        sa = jnp.sum(jnp.where(lo_half, x, 0.0), axis=1, keepdims=True)
        sb = jnp.sum(jnp.where(lo_half, 0.0, x), axis=1, keepdims=True)
        return sa, sb

    def sub_chunk(sc, carry):
        base = pl.multiple_of(sc * SCAN_SUB, SCAN_SUB)
        s = s_ref[...]
        yt_ref[:, sc] = jnp.zeros((nb,) + yt_ref.shape[2:], F32)
        for tt in range(SCAN_SUB):
            tile = pl.ds(pl.multiple_of(base + (tt // SUBLANES) * SUBLANES, SUBLANES), SUBLANES)
            sub = slice(tt % SUBLANES, tt % SUBLANES + 1)

            def rows(ref):
                return jnp.concatenate(
                    [jnp.broadcast_to(ref[b, tile, pl.ds(hp * LANES, LANES)][sub, :], (hd, LANES))
                     for b, hp in pairs], axis=0)

            def v_cols(off):
                return jnp.concatenate(
                    [vt_ref[b, sc, pl.ds(hp * LANES + off, hd), tt:tt + 1] for b, hp in pairs], axis=0)

            sa, sb = half_sums(s * rows(nkk_ref))
            s = (s * rows(w_ref)
                 + jnp.where(lo_half, sa, sb) * rows(kka_ref)
                 + jnp.where(lo_half, v_cols(0), v_cols(hd)) * rows(k_ref))
            ys = _dot3(s * rows(r_ref), ones_ref[...])
            for i, (b, hp) in enumerate(pairs):
                for lane in (tt, hd + tt):
                    yt_ref[b, sc, hp, :, lane:lane + 1] = ys[i * hd:(i + 1) * hd, lane:lane + 1]
        s_ref[...] = s
        return carry

    lax.fori_loop(0, n_sub, sub_chunk, 0)


def _rwkv_scan(r, w, k, nkk, kka, v, ct=256):
    b, t, _ = r.shape
    n_sub = ct // SCAN_SUB
    vt = v.reshape(b, t // SCAN_SUB, SCAN_SUB, D_RWKV).transpose(0, 1, 3, 2)
    n_pair = D_RWKV // LANES
    hd = RWKV_HEAD_DIM
    row_spec = pl.BlockSpec((b, ct, D_RWKV), lambda i: (0, i, 0))
    t_spec = pl.BlockSpec((b, n_sub, D_RWKV, SCAN_SUB), lambda i: (0, i, 0, 0))
    head_ones = (jnp.arange(LANES)[:, None] // hd == jnp.arange(LANES)[None, :] // hd).astype(BF16)
    yt = pl.pallas_call(
        _rwkv_scan_body,
        grid=(t // ct,),
        in_specs=[row_spec] * 5 + [t_spec, _const_spec((LANES, LANES))],
        out_specs=pl.BlockSpec((b, n_sub, n_pair, hd, LANES), lambda i: (0, i, 0, 0, 0)),
        out_shape=jax.ShapeDtypeStruct((b, t // SCAN_SUB, n_pair, hd, LANES), F32),
        scratch_shapes=[pltpu.VMEM((b * n_pair * hd, LANES), F32)],
        compiler_params=_cparams("arbitrary"),
        name="rwkv_scan",
    )(r, w, k, nkk, kka, vt, head_ones)
    y = jnp.stack([yt[..., :SCAN_SUB], yt[..., hd:hd + SCAN_SUB]], axis=3)
    return y.transpose(0, 1, 5, 2, 3, 4).reshape(b, t, D_RWKV)


def _out_proj_body(x_ref, yc_ref, yn_ref, yr_ref, bv_ref, g_ref, lg_ref, lb_ref, avg_ref,
                   wc_ref, wn_ref, wr_ref, o_ref):
    y = yr_ref[...]
    mu = _dot3(y, avg_ref[...])
    dlt = y - mu
    var = _dot3(dlt * dlt, avg_ref[...])
    yr = (dlt * lax.rsqrt(var + RWKV_GN_EPS) * lg_ref[...] + lb_ref[...] + bv_ref[...]) * g_ref[...]
    o_ref[...] = (x_ref[...]
                  + _dot(yc_ref[...].astype(BF16), wc_ref[...])
                  + _dot(yn_ref[...].astype(BF16), wn_ref[...])
                  + _dot(yr.astype(BF16), wr_ref[...]))


def _out_proj(x2d, y_conv, y_nsa, y_scan, bv, gate, ln_g, ln_b, wc, wn, wr, tm=512):
    m, d = x2d.shape
    head_avg = (jnp.arange(D_RWKV)[:, None] // RWKV_HEAD_DIM == jnp.arange(D_RWKV)[None, :] // RWKV_HEAD_DIM)
    head_avg = (head_avg.astype(F32) / RWKV_HEAD_DIM).astype(BF16)
    row_spec = lambda w: pl.BlockSpec((tm, w), lambda i: (i, 0))
    return pl.pallas_call(
        _out_proj_body,
        grid=(m // tm,),
        in_specs=[row_spec(d), row_spec(D_CONV), row_spec(D_NSA), row_spec(D_RWKV), row_spec(D_RWKV),
                  row_spec(D_RWKV), _const_spec((1, D_RWKV)), _const_spec((1, D_RWKV)),
                  _const_spec((D_RWKV, D_RWKV)), _const_spec(wc.shape), _const_spec(wn.shape),
                  _const_spec(wr.shape)],
        out_specs=row_spec(d),
        out_shape=jax.ShapeDtypeStruct((m, d), F32),
        compiler_params=_cparams("parallel"),
        name="out_proj",
    )(x2d, y_conv, y_nsa, y_scan, bv, gate, ln_g.reshape(1, -1), ln_b.reshape(1, -1), head_avg, wc, wn, wr)


def _ffn_body(x_ref, xh_ref, g_ref, wg_ref, wu_ref, cw_ref, cb_ref, wd_ref, o_ref, *, tiles_per_seq):
    tm = x_ref.shape[0]
    x = x_ref[...]
    hb = _rms(x, g_ref[...]).astype(BF16)
    hh = _rms(xh_ref[...], g_ref[...]).astype(BF16)
    a = _dot(hb, wg_ref[...])
    first = (pl.program_id(0) % tiles_per_seq) == 0
    ah = jnp.where(first, 0.0, _dot(hh, wg_ref[...]))
    row = lax.broadcasted_iota(jnp.int32, (tm, 1), 0)
    a1 = jnp.where(row == 0, ah[SUBLANES - 1:SUBLANES], pltpu.roll(a, 1, 0))
    a2 = jnp.where(row == 0, ah[SUBLANES - 2:SUBLANES - 1],
                   jnp.where(row == 1, ah[SUBLANES - 1:SUBLANES], pltpu.roll(a, 2, 0)))
    conv = cw_ref[0:1, :] * a2 + cw_ref[1:2, :] * a1 + cw_ref[2:3, :] * a + cb_ref[...]
    act = conv * _sigmoid(conv) * _dot(hb, wu_ref[...])
    o_ref[...] = x + _dot(act.astype(BF16), wd_ref[...])


def _ffn(x2d, seq, gain, wg, wu, conv_w, conv_b, wd, tm=256):
    m, d = x2d.shape
    f = wg.shape[1]
    hb = tm // SUBLANES
    cw = jnp.zeros((SUBLANES, f), F32).at[:FFN_CONV].set(conv_w)
    return pl.pallas_call(
        functools.partial(_ffn_body, tiles_per_seq=seq // tm),
        grid=(m // tm,),
        in_specs=[pl.BlockSpec((tm, d), lambda i: (i, 0)),
                  pl.BlockSpec((SUBLANES, d), lambda i: (jnp.maximum(i * hb - 1, 0), 0)),
                  _const_spec((1, d)), _const_spec(wg.shape), _const_spec(wu.shape),
                  _const_spec((SUBLANES, f)), _const_spec((1, f)), _const_spec(wd.shape)],
        out_specs=pl.BlockSpec((tm, d), lambda i: (i, 0)),
        out_shape=jax.ShapeDtypeStruct((m, d), F32),
        compiler_params=_cparams("parallel"),
        name="conv_ffn",
    )(x2d, x2d, gain.reshape(1, d), wg, wu, cw, conv_b.reshape(1, f), wd)


def _final_norm_body(x_ref, g_ref, o_ref):
    o_ref[...] = _rms(x_ref[...], g_ref[...])


def _final_norm(x2d, gain, tm=1024):
    m, d = x2d.shape
    return pl.pallas_call(
        _final_norm_body,
        grid=(m // tm,),
        in_specs=[pl.BlockSpec((tm, d), lambda i: (i, 0)), _const_spec((1, d))],
        out_specs=pl.BlockSpec((tm, d), lambda i: (i, 0)),
        out_shape=jax.ShapeDtypeStruct((m, d), F32),
        compiler_params=_cparams("parallel"),
        name="final_norm",
    )(x2d, gain.reshape(1, d))


def _split_w_in(w):
    o = 0
    wc = w[:, o:o + CONV_COLS]; o += CONV_COLS
    wq = w[:, o:o + D_NSA]; o += D_NSA
    wkv = w[:, o:o + 6 * KV_COLS]; o += 6 * KV_COLS
    wgt = w[:, o:o + 3 * NSA_HEADS]; o += 3 * NSA_HEADS
    wrkv = w[:, o:o + 3 * D_RWKV]; o += 3 * D_RWKV
    wlora = w[:, o:]
    per_group = 3 * GQ
    wgt_pad = jnp.zeros((w.shape[0], NSA_KV_HEADS * LANES), w.dtype)
    for g in range(NSA_KV_HEADS):
        wgt_pad = wgt_pad.at[:, g * LANES:g * LANES + per_group].set(wgt[:, g * per_group:(g + 1) * per_group])
    return [z.astype(BF16) for z in (wc, wq, wkv, wgt_pad, wrkv, wlora)]


def _pad_lora(w, row0, n_rows):
    return jnp.zeros((n_rows, w.shape[1]), F32).at[row0:row0 + w.shape[0]].set(w).astype(BF16)


def kernel(x, w_in, w_out, norm_mix, norm_ffn, norm_final, conv_dw_w, conv_dw_b, conv_ln_g, conv_ln_b, cmp_pe_k, cmp_pe_v, cmp_w1_k, cmp_w2_k, cmp_w1_v, cmp_w2_v, rwkv_mu, rwkv_w0, rwkv_w2, rwkv_a0, rwkv_a2, rwkv_g2, rwkv_k_k, rwkv_k_a, rwkv_r_k, rwkv_ln_g, rwkv_ln_b, ffn_w_gate, ffn_w_up, ffn_conv_w, ffn_conv_b, ffn_w_down):
    b, t, d = x.shape
    m = b * t
    n_lora = DECAY_LORA + AAA_LORA + GATE_LORA
    x2 = x.reshape(m, d)
    for i in range(w_in.shape[0]):
        p_conv, q, kv, gt, p_rkv, p_lora = _norm_proj(x2, norm_mix[i], _split_w_in(w_in[i]))
        y_conv = _conv_mixer(p_conv.reshape(b, t, -1), conv_dw_w[i], conv_dw_b[i], conv_ln_g[i], conv_ln_b[i])
        y_nsa = _nsa_mixer(q.reshape(b, t, -1), kv.reshape(b, t, -1), gt.reshape(b, t, -1),
                           cmp_pe_k[i], cmp_pe_v[i], cmp_w1_k[i], cmp_w2_k[i], cmp_w1_v[i], cmp_w2_v[i])
        r, w, k2, nkk, kka, v, bv, gate = _rwkv_prep(
            p_rkv, p_lora, t, rwkv_mu[i], rwkv_w0[i],
            _pad_lora(rwkv_w2[i], 0, n_lora), rwkv_a0[i],
            _pad_lora(rwkv_a2[i], DECAY_LORA, n_lora),
            _pad_lora(rwkv_g2[i], DECAY_LORA + AAA_LORA, n_lora),
            rwkv_k_k[i], rwkv_k_a[i], rwkv_r_k[i].reshape(-1))
        s3 = lambda z: z.reshape(b, t, D_RWKV)
        y_scan = _rwkv_scan(s3(r), s3(w), s3(k2), s3(nkk), s3(kka), s3(v))
        wo = w_out[i].astype(BF16)
        x2 = _out_proj(x2, y_conv.reshape(m, -1), y_nsa.reshape(m, -1), y_scan.reshape(m, -1), bv, gate,
                       rwkv_ln_g[i], rwkv_ln_b[i], wo[:D_CONV], wo[D_CONV:D_CONV + D_NSA], wo[D_CONV + D_NSA:])
        x2 = _ffn(x2, t, norm_ffn[i], ffn_w_gate[i].astype(BF16), ffn_w_up[i].astype(BF16),
                  ffn_conv_w[i], ffn_conv_b[i], ffn_w_down[i].astype(BF16))
    return _final_norm(x2, norm_final).reshape(b, t, d)
```

```python
import functools

import jax
import jax.numpy as jnp
from jax import lax
from jax.experimental import pallas as pl
from jax.experimental.pallas import tpu as pltpu

F32 = jnp.float32
BF16 = jnp.bfloat16

D_MODEL = 1024
DEPTH = 4
D_CONV = 256
CONV_KERNEL = 31
NSA_HEADS = 8
NSA_KV_HEADS = 2
HEAD_DIM = 64
GQ = NSA_HEADS // NSA_KV_HEADS
D_NSA = NSA_HEADS * HEAD_DIM
RWKV_HEADS = 4
RWKV_HEAD_DIM = 64
D_RWKV = RWKV_HEADS * RWKV_HEAD_DIM
CMP_BLOCK = 32
CMP_STRIDE = 16
SEL_BLOCK = 64
TOP_N = 16
WINDOW = 512
Q_BLOCK = 128
FORCE_BONUS = 1.0e4
DECAY_LORA = 32
AAA_LORA = 32
GATE_LORA = 64
RWKV_GN_EPS = 64e-5
D_FF = 2816
FFN_CONV = 3
RMS_EPS = 1e-6
LN_EPS = 1e-5
NEG_INF = -1e30
TINY = 1e-30
LOG2_E = 1.4426950408889634

CONV_COLS = 2 * D_CONV
KV_COLS = NSA_KV_HEADS * HEAD_DIM
NSA_COLS = D_NSA + 6 * KV_COLS + 3 * NSA_HEADS
RWKV_COLS = 3 * D_RWKV + DECAY_LORA + AAA_LORA + GATE_LORA

LANES = 128
SUBLANES = 8
VMEM_LIMIT_BYTES = 56 * 1024 * 1024

SEL_SUPER = LANES
SEL_CHUNK = 1024
WIN_KEYS = WINDOW + Q_BLOCK
SCAN_SUB = 32
CONV_HALO = 32

_NT = (((1,), (1,)), ((), ()))


def _cparams(*sem):
    return pltpu.CompilerParams(dimension_semantics=sem, vmem_limit_bytes=VMEM_LIMIT_BYTES)


def _const_spec(shape):
    nd = len(shape)
    return pl.BlockSpec(shape, lambda *_: (0,) * nd, pipeline_mode=pl.Buffered(1))


def _sigmoid(x):
    return 1.0 / (1.0 + jnp.exp(-x))


def _dot(a, b):
    return jnp.dot(a, b, preferred_element_type=F32)


def _dot3(a, b_exact):
    a1 = a.astype(BF16)
    r1 = a - a1.astype(F32)
    a2 = r1.astype(BF16)
    a3 = (r1 - a2.astype(F32)).astype(BF16)
    return _dot(a1, b_exact) + _dot(a2, b_exact) + _dot(a3, b_exact)


def _rms(x, g):
    return x * lax.rsqrt(jnp.mean(x * x, axis=-1, keepdims=True) + RMS_EPS) * g


def _norm_proj_body(x_ref, g_ref, *refs):
    n = len(refs) // 2
    hb = _rms(x_ref[...], g_ref[...]).astype(BF16)
    for w_ref, o_ref in zip(refs[:n], refs[n:]):
        o_ref[...] = _dot(hb, w_ref[...])


def _norm_proj(x2d, gain, ws, tm=512):
    m, d = x2d.shape
    in_specs = [pl.BlockSpec((tm, d), lambda i: (i, 0)), _const_spec((1, d))]
    in_specs += [_const_spec(w.shape) for w in ws]
    return pl.pallas_call(
        _norm_proj_body,
        grid=(m // tm,),
        in_specs=in_specs,
        out_specs=[pl.BlockSpec((tm, w.shape[1]), lambda i: (i, 0)) for w in ws],
        out_shape=[jax.ShapeDtypeStruct((m, w.shape[1]), F32) for w in ws],
        compiler_params=_cparams("parallel"),
        name="norm_proj",
    )(x2d, gain.reshape(1, d), *ws)


def _conv_body(p_ref, h_ref, w_ref, b_ref, lg_ref, lb_ref, o_ref):
    tc = p_ref.shape[1]
    ext = jnp.concatenate([h_ref[0], p_ref[0]], axis=0)
    y = ext[:, :D_CONV] * _sigmoid(ext[:, D_CONV:])
    row = lax.broadcasted_iota(jnp.int32, (CONV_HALO + tc, 1), 0)
    before_start = jnp.logical_and(pl.program_id(1) == 0, row < CONV_HALO)
    y = jnp.where(before_start, 0.0, y)
    acc = jnp.zeros((tc, D_CONV), F32) + b_ref[...]
    for j in range(CONV_KERNEL):
        s = CONV_KERNEL - 1 - j
        ys = y if s == 0 else pltpu.roll(y, s, 0)
        acc = acc + w_ref[j:j + 1, :] * ys[CONV_HALO:, :]
    mu = jnp.mean(acc, axis=-1, keepdims=True)
    dlt = acc - mu
    var = jnp.mean(dlt * dlt, axis=-1, keepdims=True)
    z = dlt * lax.rsqrt(var + LN_EPS) * lg_ref[...] + lb_ref[...]
    o_ref[0] = z * _sigmoid(z)


def _conv_mixer(p_conv, dw_w, dw_b, ln_g, ln_b, tc=512):
    b, t, _ = p_conv.shape
    hb = tc // CONV_HALO
    wpad = jnp.zeros((CONV_HALO, D_CONV), F32).at[:CONV_KERNEL].set(dw_w)
    return pl.pallas_call(
        _conv_body,
        grid=(b, t // tc),
        in_specs=[
            pl.BlockSpec((1, tc, CONV_COLS), lambda bi, i: (bi, i, 0)),
            pl.BlockSpec((1, CONV_HALO, CONV_COLS), lambda bi, i: (bi, jnp.maximum(i * hb - 1, 0), 0)),
            _const_spec((CONV_HALO, D_CONV)),
            _const_spec((1, D_CONV)),
            _const_spec((1, D_CONV)),
            _const_spec((1, D_CONV)),
        ],
        out_specs=pl.BlockSpec((1, tc, D_CONV), lambda bi, i: (bi, i, 0)),
        out_shape=jax.ShapeDtypeStruct((b, t, D_CONV), F32),
        compiler_params=_cparams("parallel", "parallel"),
        name="conv_mixer",
    )(p_conv, p_conv, wpad, dw_b.reshape(1, -1), ln_g.reshape(1, -1), ln_b.reshape(1, -1))


def _gelu_tanh(x):
    return 0.5 * x * (1.0 + jnp.tanh(0.7978845608028654 * (x + 0.044715 * (x * x * x))))


def _compress_body(xk_ref, xv_ref, pek_ref, pev_ref, w1k_ref, w2k_ref, w1v_ref, w2v_ref, ok_ref, ov_ref):
    half = CMP_STRIDE * HEAD_DIM
    nch = xk_ref.shape[1]

    def one(x_ref, pe_ref, w1_ref, w2_ref, o_ref):
        x = x_ref[0]
        a = _dot((x + pe_ref[:, :half]).astype(BF16), w1_ref[:half, :])
        bb = _dot((x + pe_ref[:, half:]).astype(BF16), w1_ref[half:, :])
        pre = a + pltpu.roll(bb, nch - 1, 0)
        o_ref[0] = _dot(_gelu_tanh(pre).astype(BF16), w2_ref[...]).astype(BF16)

    one(xk_ref, pek_ref, w1k_ref, w2k_ref, ok_ref)
    one(xv_ref, pev_ref, w1v_ref, w2v_ref, ov_ref)


def _compress(xk, xv, pe_k, pe_v, w1_k, w2_k, w1_v, w2_v):
    bg, nch, width = xk.shape
    xspec = pl.BlockSpec((1, nch, width), lambda i: (i, 0, 0))
    ospec = pl.BlockSpec((1, nch, HEAD_DIM), lambda i: (i, 0, 0))
    oshape = jax.ShapeDtypeStruct((bg, nch, HEAD_DIM), BF16)
    return pl.pallas_call(
        _compress_body,
        grid=(bg,),
        in_specs=[xspec, xspec, _const_spec((1, 2 * width)), _const_spec((1, 2 * width)),
                  _const_spec(w1_k.shape), _const_spec(w2_k.shape),
                  _const_spec(w1_v.shape), _const_spec(w2_v.shape)],
        out_specs=[ospec, ospec],
        out_shape=[oshape, oshape],
        compiler_params=_cparams("parallel"),
        name="nsa_compress",
    )(xk, xv, pe_k.reshape(1, -1), pe_v.reshape(1, -1), w1_k, w2_k, w1_v, w2_v)


def _softmax_rows(s2, mask):
    s2 = jnp.where(mask, s2, NEG_INF)
    m = jnp.max(s2, axis=-1, keepdims=True)
    e = jnp.where(mask, jnp.exp2(s2 - m), 0.0)
    return e / jnp.maximum(jnp.sum(e, axis=-1, keepdims=True), TINY)


def _nsa_body(q_ref, gt_ref, kc_ref, vc_ref, kaug_ref, vs_ref, kw_ref, vw_ref,
              o_ref, qaug_ref, sbuf_ref, impt_ref, *, nc, n_top):
    ncp = kc_ref.shape[2]
    n_super = qaug_ref.shape[0]
    nsp = n_super * SEL_SUPER
    rows = GQ * Q_BLOCK
    t0 = pl.program_id(2) * Q_BLOCK

    qf = q_ref[0] * (HEAD_DIM ** -0.5 * LOG2_E)
    q4 = jnp.concatenate([qf[:, r * HEAD_DIM:(r + 1) * HEAD_DIM] for r in range(GQ)], axis=0)
    qb = q4.astype(BF16)
    tq4 = t0 + (lax.broadcasted_iota(jnp.int32, (rows, 1), 0) & (Q_BLOCK - 1))

    s = lax.dot_general(qb, kc_ref[0, 0], _NT, preferred_element_type=F32)
    n_idx = lax.broadcasted_iota(jnp.int32, (1, ncp), 1)
    cmp_end = jnp.where(n_idx < nc, n_idx * CMP_STRIDE + (CMP_BLOCK - 1), jnp.iinfo(jnp.int32).max)
    pc = _softmax_rows(s, cmp_end <= tq4)
    o_cmp = _dot(pc.astype(BF16), vc_ref[0, 0])

    ws = pl.multiple_of(jnp.maximum(t0 - WINDOW, 0), Q_BLOCK)
    sw = lax.dot_general(qb, kw_ref[0, 0, pl.ds(ws, WIN_KEYS), :], _NT, preferred_element_type=F32)
    diff = tq4 - (ws + lax.broadcasted_iota(jnp.int32, (1, WIN_KEYS), 1))
    in_window = lax.bitcast_convert_type(diff, jnp.uint32) < jnp.uint32(WINDOW)
    sw = jnp.where(in_window, sw, NEG_INF)
    ew = jnp.exp2(sw - jnp.max(sw, axis=-1, keepdims=True))
    aw = _dot(ew.astype(BF16), vw_ref[0, 0, pl.ds(ws, WIN_KEYS), :])
    o_win = aw[:, :HEAD_DIM] / jnp.maximum(aw[:, HEAD_DIM:HEAD_DIM + 1], TINY)
    gts = _sigmoid(gt_ref[0])

    imp = pc[0:Q_BLOCK]
    for r in range(1, GQ):
        imp = imp + pc[r * Q_BLOCK:(r + 1) * Q_BLOCK]
    impt_ref[0:SUBLANES, :] = jnp.zeros((SUBLANES, Q_BLOCK), F32)
    impt_ref[SUBLANES:SUBLANES + ncp, :] = imp.T
    if impt_ref.shape[0] > SUBLANES + ncp:
        impt_ref[SUBLANES + ncp:, :] = jnp.zeros((impt_ref.shape[0] - SUBLANES - ncp, Q_BLOCK), F32)
    ratio = SEL_BLOCK // CMP_STRIDE
    sel = impt_ref[pl.ds(SUBLANES - 1, nsp, stride=ratio), :]
    for o in range(ratio):
        sel = sel + impt_ref[pl.ds(SUBLANES + o, nsp, stride=ratio), :]
    blk = lax.broadcasted_iota(jnp.int32, (nsp, 1), 0)
    blk_f = blk.astype(F32)
    cur = jnp.right_shift(t0 + lax.broadcasted_iota(jnp.int32, (1, Q_BLOCK), 1), 6)
    future = blk > cur
    forced = jnp.logical_or(blk == 0, jnp.logical_or(blk == cur, blk == cur - 1))
    work = jnp.where(forced, -3.0, jnp.where(future, -1.0, sel))
    picked = jnp.where(forced, 1.0, 0.0)
    for _ in range(n_top - 3):
        mx = jnp.max(work, axis=0, keepdims=True)
        first = jnp.min(jnp.where(work == mx, blk_f, float(nsp)), axis=0, keepdims=True)
        hit = blk_f == first
        picked = jnp.where(hit, 1.0, picked)
        work = jnp.where(hit, -3.0, work)
    keep = jnp.where(future, 0.0, picked).astype(BF16)
    eye = (lax.broadcasted_iota(jnp.int32, (Q_BLOCK, Q_BLOCK), 0)
           == lax.broadcasted_iota(jnp.int32, (Q_BLOCK, Q_BLOCK), 1)).astype(BF16)
    keep_q = lax.dot_general(eye, keep, _NT, preferred_element_type=F32)
    bias = jnp.where(keep_q > 0.5, 0.0, NEG_INF)
    bias4 = jnp.concatenate([bias] * GQ, axis=0).astype(BF16)

    for h in range(n_super):
        qaug_ref[h, :, 0:HEAD_DIM] = qb
        qaug_ref[h, :, HEAD_DIM:LANES] = jnp.zeros((rows, LANES - HEAD_DIM), BF16)
        qaug_ref[h, :, LANES:2 * LANES] = bias4[:, h * SEL_SUPER:(h + 1) * SEL_SUPER]

    chunks_per_super = SEL_SUPER * SEL_BLOCK // SEL_CHUNK

    def scores(j):
        start = pl.multiple_of(j * SEL_CHUNK, SEL_CHUNK)
        return lax.dot_general(qaug_ref[j // chunks_per_super], kaug_ref[0, 0, pl.ds(start, SEL_CHUNK), :],
                               _NT, preferred_element_type=F32)

    def accumulate(j, sc, m, acc):
        start = pl.multiple_of(j * SEL_CHUNK, SEL_CHUNK)
        m_new = jnp.maximum(m, jnp.max(sc, axis=-1, keepdims=True))
        p = jnp.exp2(sc - m_new)
        acc = jnp.exp2(m - m_new) * acc + _dot(p.astype(BF16), vs_ref[0, 0, pl.ds(start, SEL_CHUNK), :])
        return m_new, acc

    def flash_pair(i, carry):
        j = 2 * i
        sbuf_ref[1] = scores(j + 1)
        carry = accumulate(j, sbuf_ref[0], *carry)
        sbuf_ref[0] = scores(j + 2)
        return accumulate(j + 1, sbuf_ref[1], *carry)

    def diagonal(j, sc, carry):
        kpos = j * SEL_CHUNK + lax.broadcasted_iota(jnp.int32, (1, SEL_CHUNK), 1)
        return accumulate(j, jnp.where(kpos <= tq4, sc, NEG_INF), *carry)

    def tail_two(carry):
        sbuf_ref[1] = scores(j_diag)
        carry = accumulate(j_diag - 1, sbuf_ref[0], *carry)
        return diagonal(j_diag, sbuf_ref[1], carry)

    def tail_one(carry):
        return diagonal(j_diag, sbuf_ref[0], carry)

    j_diag = t0 // SEL_CHUNK
    n_pairs = j_diag // 2
    carry = (jnp.full((rows, 1), NEG_INF, F32), jnp.zeros((rows, LANES), F32))
    sbuf_ref[0] = scores(0)
    carry = lax.fori_loop(0, n_pairs, flash_pair, carry)
    _, acc_sel = lax.cond(j_diag > 2 * n_pairs, tail_two, tail_one, carry)
    o_sel = acc_sel[:, :HEAD_DIM] / jnp.maximum(acc_sel[:, HEAD_DIM:HEAD_DIM + 1], TINY)

    outs = []
    for r in range(GQ):
        sl = slice(r * Q_BLOCK, (r + 1) * Q_BLOCK)
        outs.append(gts[:, 3 * r:3 * r + 1] * o_cmp[sl]
                    + gts[:, 3 * r + 1:3 * r + 2] * o_sel[sl]
                    + gts[:, 3 * r + 2:3 * r + 3] * o_win[sl])
    o_ref[0] = jnp.concatenate(outs, axis=1)


def _nsa_attention(q, gt, k_cmp, v_cmp, k_aug, v_sel, k_win, v_win):
    b, t, _ = q.shape
    g = NSA_KV_HEADS
    ncp = k_cmp.shape[2]
    nc = t // CMP_STRIDE - CMP_BLOCK // CMP_STRIDE + 1
    ns = t // SEL_BLOCK
    nsp = -(-ns // SEL_SUPER) * SEL_SUPER
    impt_rows = SUBLANES + max(ncp, nsp * (SEL_BLOCK // CMP_STRIDE))
    body = functools.partial(_nsa_body, nc=nc, n_top=min(TOP_N, ns))

    def resident(shape):
        return pl.BlockSpec((1, 1) + shape, lambda bi, gi, c: (bi, gi, 0, 0), pipeline_mode=pl.Buffered(1))

    return pl.pallas_call(
        body,
        grid=(b, g, t // Q_BLOCK),
        in_specs=[
            pl.BlockSpec((1, Q_BLOCK, GQ * HEAD_DIM), lambda bi, gi, c: (bi, c, gi)),
            pl.BlockSpec((1, Q_BLOCK, LANES), lambda bi, gi, c: (bi, c, gi)),
            resident((ncp, HEAD_DIM)),
            resident((ncp, HEAD_DIM)),
            resident((t, 2 * LANES)),
            resident((t, LANES)),
            resident((t, HEAD_DIM)),
            resident((t, LANES)),
        ],
        out_specs=pl.BlockSpec((1, Q_BLOCK, GQ * HEAD_DIM), lambda bi, gi, c: (bi, c, gi)),
        out_shape=jax.ShapeDtypeStruct((b, t, D_NSA), F32),
        scratch_shapes=[pltpu.VMEM((nsp // SEL_SUPER, GQ * Q_BLOCK, 2 * LANES), BF16),
                        pltpu.VMEM((2, GQ * Q_BLOCK, SEL_CHUNK), F32),
                        pltpu.VMEM((impt_rows, Q_BLOCK), F32)],
        compiler_params=_cparams("arbitrary", "arbitrary", "arbitrary"),
        name="nsa_attention",
    )(q, gt, k_cmp, v_cmp, k_aug, v_sel, k_win, v_win)


def _nsa_mixer(q, kv, gt, pe_k, pe_v, w1_k, w2_k, w1_v, w2_v):
    b, t, _ = q.shape
    g = NSA_KV_HEADS
    nch = t // CMP_STRIDE
    nc = nch - CMP_BLOCK // CMP_STRIDE + 1
    ns = t // SEL_BLOCK
    nsp = -(-ns // SEL_SUPER) * SEL_SUPER

    def heads(x):
        return x.reshape(b, t, g, HEAD_DIM).transpose(0, 2, 1, 3)

    def chunks(x):
        return heads(x).reshape(b * g, nch, CMP_STRIDE * HEAD_DIM)

    kc, vc, ks, vs, kw, vw = [kv[..., i * KV_COLS:(i + 1) * KV_COLS] for i in range(6)]
    k_cmp, v_cmp = _compress(chunks(kc), chunks(vc), pe_k, pe_v,
                             w1_k.astype(BF16), w2_k.astype(BF16), w1_v.astype(BF16), w2_v.astype(BF16))
    k_cmp = k_cmp.reshape(b, g, nch, HEAD_DIM)
    v_cmp = v_cmp.reshape(b, g, nch, HEAD_DIM)

    onehot = (jnp.arange(t)[:, None] // SEL_BLOCK) % SEL_SUPER == jnp.arange(SEL_SUPER)[None, :]
    k_aug = jnp.concatenate([
        heads(ks).astype(BF16),
        jnp.zeros((b, g, t, LANES - HEAD_DIM), BF16),
        jnp.broadcast_to(onehot.astype(BF16), (b, g, t, SEL_SUPER)),
    ], axis=-1)
    ones_col = (jnp.arange(LANES - HEAD_DIM) == 0).astype(BF16)

    def with_ones(x):
        return jnp.concatenate([heads(x).astype(BF16), jnp.broadcast_to(ones_col, (b, g, t, LANES - HEAD_DIM))],
                               axis=-1)

    return _nsa_attention(q, gt, k_cmp, v_cmp, k_aug, with_ones(vs),
                          heads(kw).astype(BF16), with_ones(vw))


def _rwkv_prep_body(p_ref, ph_ref, lo_ref, loh_ref, mu_ref, mul_ref, w0_ref, w2_ref, a0_ref, a2_ref,
                    g2_ref, kk_ref, ka_ref, rk_ref, ones_ref,
                    r_o, w_o, k_o, nkk_o, kka_o, v_o, bv_o, g_o, *, tiles_per_seq):
    tm = p_ref.shape[0]
    first = (pl.program_id(0) % tiles_per_seq) == 0
    row = lax.broadcasted_iota(jnp.int32, (tm, 1), 0)

    def shifted(x_ref, h_ref, m_ref):
        x = x_ref[...]
        last = jnp.where(first, 0.0, h_ref[SUBLANES - 1:SUBLANES, :])
        prev = jnp.where(row == 0, last, pltpu.roll(x, 1, 0))
        return x + (prev - x) * m_ref[...]

    ps = shifted(p_ref, ph_ref, mu_ref)
    lo = shifted(lo_ref, loh_ref, mul_ref)
    r = ps[:, 0:D_RWKV]
    k = ps[:, D_RWKV:2 * D_RWKV]
    v = ps[:, 2 * D_RWKV:3 * D_RWKV]

    z = w0_ref[...] + _dot(jnp.tanh(lo).astype(BF16), w2_ref[...])
    softplus_neg = jnp.maximum(-z, 0.0) + jnp.log(1.0 + jnp.exp(-jnp.abs(z)))
    decay = jnp.exp(-jnp.exp(-softplus_neg - 0.5))
    a = _sigmoid(a0_ref[...] + _dot(lo.astype(BF16), a2_ref[...]))
    gate = _dot(_sigmoid(lo).astype(BF16), g2_ref[...])

    kk = k * kk_ref[...]
    kk = kk / jnp.maximum(jnp.sqrt(_dot3(kk * kk, ones_ref[...])), 1e-12)
    k2 = k * (1.0 + (a - 1.0) * ka_ref[...])
    bonus = _dot3(r * k2 * rk_ref[...], ones_ref[...])

    r_o[...] = r
    w_o[...] = decay
    k_o[...] = k2
    nkk_o[...] = -kk
    kka_o[...] = kk * a
    v_o[...] = v
    bv_o[...] = bonus * v
    g_o[...] = gate


def _rwkv_prep(p_rkv, p_lora, seq, mu, w0, w2p, a0, a2p, g2p, k_k, k_a, r_k, tm=256):
    m = p_rkv.shape[0]
    hb = tm // SUBLANES
    head_ones = (jnp.arange(D_RWKV)[:, None] // RWKV_HEAD_DIM == jnp.arange(D_RWKV)[None, :] // RWKV_HEAD_DIM)
    row_spec = lambda w: pl.BlockSpec((tm, w), lambda i: (i, 0))
    halo_spec = lambda w: pl.BlockSpec((SUBLANES, w), lambda i: (jnp.maximum(i * hb - 1, 0), 0))
    vec = lambda x: x.reshape(1, -1)
    n_lora = p_lora.shape[1]
    oshape = jax.ShapeDtypeStruct((m, D_RWKV), F32)
    return pl.pallas_call(
        functools.partial(_rwkv_prep_body, tiles_per_seq=seq // tm),
        grid=(m // tm,),
        in_specs=[row_spec(3 * D_RWKV), halo_spec(3 * D_RWKV), row_spec(n_lora), halo_spec(n_lora),
                  _const_spec((1, 3 * D_RWKV)), _const_spec((1, n_lora)),
                  _const_spec((1, D_RWKV)), _const_spec((n_lora, D_RWKV)),
                  _const_spec((1, D_RWKV)), _const_spec((n_lora, D_RWKV)),
                  _const_spec((n_lora, D_RWKV)),
                  _const_spec((1, D_RWKV)), _const_spec((1, D_RWKV)), _const_spec((1, D_RWKV)),
                  _const_spec((D_RWKV, D_RWKV))],
        out_specs=[row_spec(D_RWKV)] * 8,
        out_shape=[oshape] * 8,
        compiler_params=_cparams("parallel"),
        name="rwkv_prep",
    )(p_rkv, p_rkv, p_lora, p_lora, vec(mu[:3 * D_RWKV]), vec(mu[3 * D_RWKV:]), vec(w0), w2p,
      vec(a0), a2p, g2p, vec(k_k), vec(k_a), vec(r_k), head_ones.astype(BF16))


def _rwkv_scan_body(r_ref, w_ref, k_ref, nkk_ref, kka_ref, vt_ref, ones_ref, yt_ref, s_ref, vbuf_ref):
    nb = r_ref.shape[0]
    n_sub = vt_ref.shape[1]
    hd = RWKV_HEAD_DIM
    pairs = [(b, hp) for b in range(nb) for hp in range(D_RWKV // LANES)]
    n_rows = len(pairs) * hd
    lo_half = lax.broadcasted_iota(jnp.int32, (n_rows, LANES), 1) < hd

    @pl.when(pl.program_id(0) == 0)
    def _():
        s_ref[...] = jnp.zeros_like(s_ref)

    def half_sums(x):
        sa = jnp.sum(jnp.where(lo_half, x, 0.0), axis=1, keepdims=True)
        sb = jnp.sum(jnp.where(lo_half, 0.0, x), axis=1, keepdims=True)
        return sa, sb

    def sub_chunk(sc, carry):
        base = pl.multiple_of(sc * SCAN_SUB, SCAN_SUB)
        s = s_ref[...]
        yt_ref[:, sc] = jnp.zeros((nb,) + yt_ref.shape[2:], F32)

        def v_tile(tt):
            src = (lax.broadcasted_iota(jnp.int32, (hd, LANES), 1) & hd) + tt
            return jnp.concatenate([jnp.take_along_axis(vt_ref[b, sc, hp], src, axis=1) for b, hp in pairs],
                                   axis=0)

        vbuf_ref[0] = v_tile(0)
        for tt in range(SCAN_SUB):
            tile = pl.ds(pl.multiple_of(base + (tt // SUBLANES) * SUBLANES, SUBLANES), SUBLANES)
            sub = slice(tt % SUBLANES, tt % SUBLANES + 1)

            def rows(ref):
                return jnp.concatenate(
                    [jnp.broadcast_to(ref[b, tile, pl.ds(hp * LANES, LANES)][sub, :], (hd, LANES))
                     for b, hp in pairs], axis=0)

            if tt + 1 < SCAN_SUB:
                vbuf_ref[(tt + 1) % 2] = v_tile(tt + 1)
            sa, sb = half_sums(s * rows(nkk_ref))
            s = (s * rows(w_ref)
                 + jnp.where(lo_half, sa, sb) * rows(kka_ref)
                 + vbuf_ref[tt % 2] * rows(k_ref))
            ys = _dot3(s * rows(r_ref), ones_ref[...])
            for i, (b, hp) in enumerate(pairs):
                for lane in (tt, hd + tt):
                    yt_ref[b, sc, hp, :, lane:lane + 1] = ys[i * hd:(i + 1) * hd, lane:lane + 1]
        s_ref[...] = s
        return carry

    lax.fori_loop(0, n_sub, sub_chunk, 0)


def _rwkv_scan(r, w, k, nkk, kka, v, ct=256):
    b, t, _ = r.shape
    n_sub = ct // SCAN_SUB
    n_pair = D_RWKV // LANES
    hd = RWKV_HEAD_DIM
    vt = v.reshape(b, t // SCAN_SUB, SCAN_SUB, n_pair, 2, hd).transpose(0, 1, 3, 5, 4, 2)
    vt = jnp.pad(vt, ((0, 0),) * 5 + ((0, hd - SCAN_SUB),)).reshape(b, t // SCAN_SUB, n_pair, hd, LANES)
    row_spec = pl.BlockSpec((b, ct, D_RWKV), lambda i: (0, i, 0))
    t_spec = pl.BlockSpec((b, n_sub, n_pair, hd, LANES), lambda i: (0, i, 0, 0, 0))
    head_ones = (jnp.arange(LANES)[:, None] // hd == jnp.arange(LANES)[None, :] // hd).astype(BF16)
    yt = pl.pallas_call(
        _rwkv_scan_body,
        grid=(t // ct,),
        in_specs=[row_spec] * 5 + [t_spec, _const_spec((LANES, LANES))],
        out_specs=pl.BlockSpec((b, n_sub, n_pair, hd, LANES), lambda i: (0, i, 0, 0, 0)),
        out_shape=jax.ShapeDtypeStruct((b, t // SCAN_SUB, n_pair, hd, LANES), F32),
        scratch_shapes=[pltpu.VMEM((b * n_pair * hd, LANES), F32),
                        pltpu.VMEM((2, b * n_pair * hd, LANES), F32)],
        compiler_params=_cparams("arbitrary"),
        name="rwkv_scan",
    )(r, w, k, nkk, kka, vt, head_ones)
    y = jnp.stack([yt[..., :SCAN_SUB], yt[..., hd:hd + SCAN_SUB]], axis=3)
    return y.transpose(0, 1, 5, 2, 3, 4).reshape(b, t, D_RWKV)


def _out_proj_body(x_ref, yc_ref, yn_ref, yr_ref, bv_ref, g_ref, lg_ref, lb_ref, avg_ref,
                   wc_ref, wn_ref, wr_ref, o_ref):
    y = yr_ref[...]
    mu = _dot3(y, avg_ref[...])
    dlt = y - mu
    var = _dot3(dlt * dlt, avg_ref[...])
    yr = (dlt * lax.rsqrt(var + RWKV_GN_EPS) * lg_ref[...] + lb_ref[...] + bv_ref[...]) * g_ref[...]
    o_ref[...] = (x_ref[...]
                  + _dot(yc_ref[...].astype(BF16), wc_ref[...])
                  + _dot(yn_ref[...].astype(BF16), wn_ref[...])
                  + _dot(yr.astype(BF16), wr_ref[...]))


def _out_proj(x2d, y_conv, y_nsa, y_scan, bv, gate, ln_g, ln_b, wc, wn, wr, tm=512):
    m, d = x2d.shape
    head_avg = (jnp.arange(D_RWKV)[:, None] // RWKV_HEAD_DIM == jnp.arange(D_RWKV)[None, :] // RWKV_HEAD_DIM)
    head_avg = (head_avg.astype(F32) / RWKV_HEAD_DIM).astype(BF16)
    row_spec = lambda w: pl.BlockSpec((tm, w), lambda i: (i, 0))
    return pl.pallas_call(
        _out_proj_body,
        grid=(m // tm,),
        in_specs=[row_spec(d), row_spec(D_CONV), row_spec(D_NSA), row_spec(D_RWKV), row_spec(D_RWKV),
                  row_spec(D_RWKV), _const_spec((1, D_RWKV)), _const_spec((1, D_RWKV)),
                  _const_spec((D_RWKV, D_RWKV)), _const_spec(wc.shape), _const_spec(wn.shape),
                  _const_spec(wr.shape)],
        out_specs=row_spec(d),
        out_shape=jax.ShapeDtypeStruct((m, d), F32),
        compiler_params=_cparams("parallel"),
        name="out_proj",
    )(x2d, y_conv, y_nsa, y_scan, bv, gate, ln_g.reshape(1, -1), ln_b.reshape(1, -1), head_avg, wc, wn, wr)


def _ffn_body(x_ref, xh_ref, g_ref, wg_ref, wu_ref, cw_ref, cb_ref, wd_ref, o_ref, *, tiles_per_seq):
    tm = x_ref.shape[0]
    x = x_ref[...]
    hb = _rms(x, g_ref[...]).astype(BF16)
    hh = _rms(xh_ref[...], g_ref[...]).astype(BF16)
    a = _dot(hb, wg_ref[...])
    first = (pl.program_id(0) % tiles_per_seq) == 0
    ah = jnp.where(first, 0.0, _dot(hh, wg_ref[...]))
    row = lax.broadcasted_iota(jnp.int32, (tm, 1), 0)
    a1 = jnp.where(row == 0, ah[SUBLANES - 1:SUBLANES], pltpu.roll(a, 1, 0))
    a2 = jnp.where(row == 0, ah[SUBLANES - 2:SUBLANES - 1],
                   jnp.where(row == 1, ah[SUBLANES - 1:SUBLANES], pltpu.roll(a, 2, 0)))
    conv = cw_ref[0:1, :] * a2 + cw_ref[1:2, :] * a1 + cw_ref[2:3, :] * a + cb_ref[...]
    act = conv * _sigmoid(conv) * _dot(hb, wu_ref[...])
    o_ref[...] = x + _dot(act.astype(BF16), wd_ref[...])


def _ffn(x2d, seq, gain, wg, wu, conv_w, conv_b, wd, tm=256):
    m, d = x2d.shape
    f = wg.shape[1]
    hb = tm // SUBLANES
    cw = jnp.zeros((SUBLANES, f), F32).at[:FFN_CONV].set(conv_w)
    return pl.pallas_call(
        functools.partial(_ffn_body, tiles_per_seq=seq // tm),
        grid=(m // tm,),
        in_specs=[pl.BlockSpec((tm, d), lambda i: (i, 0)),
                  pl.BlockSpec((SUBLANES, d), lambda i: (jnp.maximum(i * hb - 1, 0), 0)),
                  _const_spec((1, d)), _const_spec(wg.shape), _const_spec(wu.shape),
                  _const_spec((SUBLANES, f)), _const_spec((1, f)), _const_spec(wd.shape)],
        out_specs=pl.BlockSpec((tm, d), lambda i: (i, 0)),
        out_shape=jax.ShapeDtypeStruct((m, d), F32),
        compiler_params=_cparams("parallel"),
        name="conv_ffn",
    )(x2d, x2d, gain.reshape(1, d), wg, wu, cw, conv_b.reshape(1, f), wd)


def _final_norm_body(x_ref, g_ref, o_ref):
    o_ref[...] = _rms(x_ref[...], g_ref[...])


def _final_norm(x2d, gain, tm=1024):
    m, d = x2d.shape
    return pl.pallas_call(
        _final_norm_body,
        grid=(m // tm,),
        in_specs=[pl.BlockSpec((tm, d), lambda i: (i, 0)), _const_spec((1, d))],
        out_specs=pl.BlockSpec((tm, d), lambda i: (i, 0)),
        out_shape=jax.ShapeDtypeStruct((m, d), F32),
        compiler_params=_cparams("parallel"),
        name="final_norm",
    )(x2d, gain.reshape(1, d))


def _split_w_in(w):
    o = 0
    wc = w[:, o:o + CONV_COLS]; o += CONV_COLS
    wq = w[:, o:o + D_NSA]; o += D_NSA
    wkv = w[:, o:o + 6 * KV_COLS]; o += 6 * KV_COLS
    wgt = w[:, o:o + 3 * NSA_HEADS]; o += 3 * NSA_HEADS
    wrkv = w[:, o:o + 3 * D_RWKV]; o += 3 * D_RWKV
    wlora = w[:, o:]
    per_group = 3 * GQ
    wgt_pad = jnp.zeros((w.shape[0], NSA_KV_HEADS * LANES), w.dtype)
    for g in range(NSA_KV_HEADS):
        wgt_pad = wgt_pad.at[:, g * LANES:g * LANES + per_group].set(wgt[:, g * per_group:(g + 1) * per_group])
    return [z.astype(BF16) for z in (wc, wq, wkv, wgt_pad, wrkv, wlora)]


def _pad_lora(w, row0, n_rows):
    return jnp.zeros((n_rows, w.shape[1]), F32).at[row0:row0 + w.shape[0]].set(w).astype(BF16)


def kernel(x, w_in, w_out, norm_mix, norm_ffn, norm_final, conv_dw_w, conv_dw_b, conv_ln_g, conv_ln_b, cmp_pe_k, cmp_pe_v, cmp_w1_k, cmp_w2_k, cmp_w1_v, cmp_w2_v, rwkv_mu, rwkv_w0, rwkv_w2, rwkv_a0, rwkv_a2, rwkv_g2, rwkv_k_k, rwkv_k_a, rwkv_r_k, rwkv_ln_g, rwkv_ln_b, ffn_w_gate, ffn_w_up, ffn_conv_w, ffn_conv_b, ffn_w_down):
    b, t, d = x.shape
    m = b * t
    n_lora = DECAY_LORA + AAA_LORA + GATE_LORA
    x2 = x.reshape(m, d)
    for i in range(w_in.shape[0]):
        p_conv, q, kv, gt, p_rkv, p_lora = _norm_proj(x2, norm_mix[i], _split_w_in(w_in[i]))
        y_conv = _conv_mixer(p_conv.reshape(b, t, -1), conv_dw_w[i], conv_dw_b[i], conv_ln_g[i], conv_ln_b[i])
        y_nsa = _nsa_mixer(q.reshape(b, t, -1), kv.reshape(b, t, -1), gt.reshape(b, t, -1),
                           cmp_pe_k[i], cmp_pe_v[i], cmp_w1_k[i], cmp_w2_k[i], cmp_w1_v[i], cmp_w2_v[i])
        r, w, k2, nkk, kka, v, bv, gate = _rwkv_prep(
            p_rkv, p_lora, t, rwkv_mu[i], rwkv_w0[i],
            _pad_lora(rwkv_w2[i], 0, n_lora), rwkv_a0[i],
            _pad_lora(rwkv_a2[i], DECAY_LORA, n_lora),
            _pad_lora(rwkv_g2[i], DECAY_LORA + AAA_LORA, n_lora),
            rwkv_k_k[i], rwkv_k_a[i], rwkv_r_k[i].reshape(-1))
        s3 = lambda z: z.reshape(b, t, D_RWKV)
        y_scan = _rwkv_scan(s3(r), s3(w), s3(k2), s3(nkk), s3(kka), s3(v))
        wo = w_out[i].astype(BF16)
        x2 = _out_proj(x2, y_conv.reshape(m, -1), y_nsa.reshape(m, -1), y_scan.reshape(m, -1), bv, gate,
                       rwkv_ln_g[i], rwkv_ln_b[i], wo[:D_CONV], wo[D_CONV:D_CONV + D_NSA], wo[D_CONV + D_NSA:])
        x2 = _ffn(x2, t, norm_ffn[i], ffn_w_gate[i].astype(BF16), ffn_w_up[i].astype(BF16),
                  ffn_conv_w[i], ffn_conv_b[i], ffn_w_down[i].astype(BF16))
    return _final_norm(x2, norm_final).reshape(b, t, d)
```

```python
import functools

import jax
import jax.numpy as jnp
from jax import lax
from jax.experimental import pallas as pl
from jax.experimental.pallas import tpu as pltpu

F32 = jnp.float32
BF16 = jnp.bfloat16

D_MODEL = 1024
DEPTH = 4
D_CONV = 256
CONV_KERNEL = 31
NSA_HEADS = 8
NSA_KV_HEADS = 2
HEAD_DIM = 64
GQ = NSA_HEADS // NSA_KV_HEADS
D_NSA = NSA_HEADS * HEAD_DIM
RWKV_HEADS = 4
RWKV_HEAD_DIM = 64
D_RWKV = RWKV_HEADS * RWKV_HEAD_DIM
CMP_BLOCK = 32
CMP_STRIDE = 16
SEL_BLOCK = 64
TOP_N = 16
WINDOW = 512
Q_BLOCK = 128
FORCE_BONUS = 1.0e4
DECAY_LORA = 32
AAA_LORA = 32
GATE_LORA = 64
RWKV_GN_EPS = 64e-5
D_FF = 2816
FFN_CONV = 3
RMS_EPS = 1e-6
LN_EPS = 1e-5
NEG_INF = -1e30
TINY = 1e-30
LOG2_E = 1.4426950408889634

CONV_COLS = 2 * D_CONV
KV_COLS = NSA_KV_HEADS * HEAD_DIM
NSA_COLS = D_NSA + 6 * KV_COLS + 3 * NSA_HEADS
RWKV_COLS = 3 * D_RWKV + DECAY_LORA + AAA_LORA + GATE_LORA

LANES = 128
SUBLANES = 8
VMEM_LIMIT_BYTES = 56 * 1024 * 1024

SEL_SUPER = LANES
SEL_CHUNK = 1024
WIN_KEYS = WINDOW + Q_BLOCK
SCAN_SUB = RWKV_HEAD_DIM
CONV_HALO = 32

_NT = (((1,), (1,)), ((), ()))


def _cparams(*sem):
    return pltpu.CompilerParams(dimension_semantics=sem, vmem_limit_bytes=VMEM_LIMIT_BYTES)


def _const_spec(shape):
    nd = len(shape)
    return pl.BlockSpec(shape, lambda *_: (0,) * nd, pipeline_mode=pl.Buffered(1))


def _sigmoid(x):
    return 1.0 / (1.0 + jnp.exp(-x))


def _dot(a, b):
    return jnp.dot(a, b, preferred_element_type=F32)


def _dot3(a, b_exact):
    a1 = a.astype(BF16)
    r1 = a - a1.astype(F32)
    a2 = r1.astype(BF16)
    a3 = (r1 - a2.astype(F32)).astype(BF16)
    return _dot(a1, b_exact) + _dot(a2, b_exact) + _dot(a3, b_exact)


def _rms(x, g):
    return x * lax.rsqrt(jnp.mean(x * x, axis=-1, keepdims=True) + RMS_EPS) * g


def _norm_proj_body(x_ref, g_ref, *refs, n_flat, tiles_per_seq):
    tm = x_ref.shape[0]
    hb = _rms(x_ref[...], g_ref[...]).astype(BF16)
    w_flat, (wkc, wvc, wks, wvs, wkw, wvw) = refs[:n_flat], refs[n_flat:n_flat + 6]
    o_flat, (okc, ovc, okaug, ovaug, okw, ovwaug) = refs[n_flat + 6:2 * n_flat + 6], refs[2 * n_flat + 6:]
    for w_ref, o_ref in zip(w_flat, o_flat):
        o_ref[...] = _dot(hb, w_ref[...])
    pos = (pl.program_id(0) % tiles_per_seq) * tm + lax.broadcasted_iota(jnp.int32, (tm, 1), 0)
    lane = lax.broadcasted_iota(jnp.int32, (1, LANES), 1)
    onehot = (lane == (jnp.right_shift(pos, 6) & (SEL_SUPER - 1))).astype(BF16)
    ones_col = (lane == HEAD_DIM).astype(F32)
    for g in range(NSA_KV_HEADS):
        okc[g] = _dot(hb, wkc[g])
        ovc[g] = _dot(hb, wvc[g])
        okaug[g] = jnp.concatenate([_dot(hb, wks[g]).astype(BF16), onehot], axis=1)
        ovaug[g] = (_dot(hb, wvs[g]) + ones_col).astype(BF16)
        okw[g] = _dot(hb, wkw[g]).astype(BF16)
        ovwaug[g] = (_dot(hb, wvw[g]) + ones_col).astype(BF16)


def _norm_proj(x2d, seq, gain, w_flat, w_kv, tm=512):
    m, d = x2d.shape
    g = NSA_KV_HEADS
    in_specs = [pl.BlockSpec((tm, d), lambda i: (i, 0)), _const_spec((1, d))]
    in_specs += [_const_spec(w.shape) for w in w_flat + w_kv]
    kv_out = [(HEAD_DIM, F32), (HEAD_DIM, F32), (2 * LANES, BF16), (LANES, BF16), (HEAD_DIM, BF16), (LANES, BF16)]
    out_specs = [pl.BlockSpec((tm, w.shape[1]), lambda i: (i, 0)) for w in w_flat]
    out_specs += [pl.BlockSpec((g, tm, width), lambda i: (0, i, 0)) for width, _ in kv_out]
    out_shape = [jax.ShapeDtypeStruct((m, w.shape[1]), F32) for w in w_flat]
    out_shape += [jax.ShapeDtypeStruct((g, m, width), dt) for width, dt in kv_out]
    return pl.pallas_call(
        functools.partial(_norm_proj_body, n_flat=len(w_flat), tiles_per_seq=seq // tm),
        grid=(m // tm,),
        in_specs=in_specs,
        out_specs=out_specs,
        out_shape=out_shape,
        compiler_params=_cparams("parallel"),
        name="norm_proj",
    )(x2d, gain.reshape(1, d), *w_flat, *w_kv)


def _conv_body(p_ref, h_ref, w_ref, b_ref, lg_ref, lb_ref, o_ref):
    tc = p_ref.shape[1]
    ext = jnp.concatenate([h_ref[0], p_ref[0]], axis=0)
    y = ext[:, :D_CONV] * _sigmoid(ext[:, D_CONV:])
    row = lax.broadcasted_iota(jnp.int32, (CONV_HALO + tc, 1), 0)
    before_start = jnp.logical_and(pl.program_id(1) == 0, row < CONV_HALO)
    y = jnp.where(before_start, 0.0, y)
    acc = jnp.zeros((tc, D_CONV), F32) + b_ref[...]
    for j in range(CONV_KERNEL):
        s = CONV_KERNEL - 1 - j
        ys = y if s == 0 else pltpu.roll(y, s, 0)
        acc = acc + w_ref[j:j + 1, :] * ys[CONV_HALO:, :]
    mu = jnp.mean(acc, axis=-1, keepdims=True)
    dlt = acc - mu
    var = jnp.mean(dlt * dlt, axis=-1, keepdims=True)
    z = dlt * lax.rsqrt(var + LN_EPS) * lg_ref[...] + lb_ref[...]
    o_ref[0] = z * _sigmoid(z)


def _conv_mixer(p_conv, dw_w, dw_b, ln_g, ln_b, tc=512):
    b, t, _ = p_conv.shape
    hb = tc // CONV_HALO
    wpad = jnp.zeros((CONV_HALO, D_CONV), F32).at[:CONV_KERNEL].set(dw_w)
    return pl.pallas_call(
        _conv_body,
        grid=(b, t // tc),
        in_specs=[
            pl.BlockSpec((1, tc, CONV_COLS), lambda bi, i: (bi, i, 0)),
            pl.BlockSpec((1, CONV_HALO, CONV_COLS), lambda bi, i: (bi, jnp.maximum(i * hb - 1, 0), 0)),
            _const_spec((CONV_HALO, D_CONV)),
            _const_spec((1, D_CONV)),
            _const_spec((1, D_CONV)),
            _const_spec((1, D_CONV)),
        ],
        out_specs=pl.BlockSpec((1, tc, D_CONV), lambda bi, i: (bi, i, 0)),
        out_shape=jax.ShapeDtypeStruct((b, t, D_CONV), F32),
        compiler_params=_cparams("parallel", "parallel"),
        name="conv_mixer",
    )(p_conv, p_conv, wpad, dw_b.reshape(1, -1), ln_g.reshape(1, -1), ln_b.reshape(1, -1))


def _gelu_tanh(x):
    return 0.5 * x * (1.0 + jnp.tanh(0.7978845608028654 * (x + 0.044715 * (x * x * x))))


def _compress_body(xk_ref, xv_ref, pek_ref, pev_ref, w1k_ref, w2k_ref, w1v_ref, w2v_ref, ok_ref, ov_ref):
    half = CMP_STRIDE * HEAD_DIM
    nch = xk_ref.shape[1]

    def one(x_ref, pe_ref, w1_ref, w2_ref, o_ref):
        x = x_ref[0]
        a = _dot((x + pe_ref[:, :half]).astype(BF16), w1_ref[:half, :])
        bb = _dot((x + pe_ref[:, half:]).astype(BF16), w1_ref[half:, :])
        pre = a + pltpu.roll(bb, nch - 1, 0)
        o_ref[0] = _dot(_gelu_tanh(pre).astype(BF16), w2_ref[...]).astype(BF16)

    one(xk_ref, pek_ref, w1k_ref, w2k_ref, ok_ref)
    one(xv_ref, pev_ref, w1v_ref, w2v_ref, ov_ref)


def _compress(xk, xv, pe_k, pe_v, w1_k, w2_k, w1_v, w2_v):
    bg, nch, width = xk.shape
    xspec = pl.BlockSpec((1, nch, width), lambda i: (i, 0, 0))
    ospec = pl.BlockSpec((1, nch, HEAD_DIM), lambda i: (i, 0, 0))
    oshape = jax.ShapeDtypeStruct((bg, nch, HEAD_DIM), BF16)
    return pl.pallas_call(
        _compress_body,
        grid=(bg,),
        in_specs=[xspec, xspec, _const_spec((1, 2 * width)), _const_spec((1, 2 * width)),
                  _const_spec(w1_k.shape), _const_spec(w2_k.shape),
                  _const_spec(w1_v.shape), _const_spec(w2_v.shape)],
        out_specs=[ospec, ospec],
        out_shape=[oshape, oshape],
        compiler_params=_cparams("parallel"),
        name="nsa_compress",
    )(xk, xv, pe_k.reshape(1, -1), pe_v.reshape(1, -1), w1_k, w2_k, w1_v, w2_v)


def _softmax_rows(s2, mask):
    s2 = jnp.where(mask, s2, NEG_INF)
    m = jnp.max(s2, axis=-1, keepdims=True)
    e = jnp.where(mask, jnp.exp2(s2 - m), 0.0)
    return e / jnp.maximum(jnp.sum(e, axis=-1, keepdims=True), TINY)


def _nsa_body(q_ref, gt_ref, kc_ref, vc_ref, kaug_ref, vs_ref, kw_ref, vw_ref,
              o_ref, qaug_ref, sbuf_ref, impt_ref, *, nc, n_top):
    ncp = kc_ref.shape[2]
    n_super = qaug_ref.shape[0]
    nsp = n_super * SEL_SUPER
    rows = GQ * Q_BLOCK
    t0 = pl.program_id(2) * Q_BLOCK

    qf = q_ref[0] * (HEAD_DIM ** -0.5 * LOG2_E)
    q4 = jnp.concatenate([qf[:, r * HEAD_DIM:(r + 1) * HEAD_DIM] for r in range(GQ)], axis=0)
    qb = q4.astype(BF16)
    tq4 = t0 + (lax.broadcasted_iota(jnp.int32, (rows, 1), 0) & (Q_BLOCK - 1))

    s = lax.dot_general(qb, kc_ref[0, 0], _NT, preferred_element_type=F32)
    n_idx = lax.broadcasted_iota(jnp.int32, (1, ncp), 1)
    cmp_end = jnp.where(n_idx < nc, n_idx * CMP_STRIDE + (CMP_BLOCK - 1), jnp.iinfo(jnp.int32).max)
    pc = _softmax_rows(s, cmp_end <= tq4)
    o_cmp = _dot(pc.astype(BF16), vc_ref[0, 0])

    ws = pl.multiple_of(jnp.maximum(t0 - WINDOW, 0), Q_BLOCK)
    sw = lax.dot_general(qb, kw_ref[0, 0, pl.ds(ws, WIN_KEYS), :], _NT, preferred_element_type=F32)
    diff = tq4 - (ws + lax.broadcasted_iota(jnp.int32, (1, WIN_KEYS), 1))
    in_window = lax.bitcast_convert_type(diff, jnp.uint32) < jnp.uint32(WINDOW)
    sw = jnp.where(in_window, sw, NEG_INF)
    ew = jnp.exp2(sw - jnp.max(sw, axis=-1, keepdims=True))
    aw = _dot(ew.astype(BF16), vw_ref[0, 0, pl.ds(ws, WIN_KEYS), :])
    o_win = aw[:, :HEAD_DIM] / jnp.maximum(aw[:, HEAD_DIM:HEAD_DIM + 1], TINY)
    gts = _sigmoid(gt_ref[0])

    imp = pc[0:Q_BLOCK]
    for r in range(1, GQ):
        imp = imp + pc[r * Q_BLOCK:(r + 1) * Q_BLOCK]
    impt_ref[0:SUBLANES, :] = jnp.zeros((SUBLANES, Q_BLOCK), F32)
    impt_ref[SUBLANES:SUBLANES + ncp, :] = imp.T
    if impt_ref.shape[0] > SUBLANES + ncp:
        impt_ref[SUBLANES + ncp:, :] = jnp.zeros((impt_ref.shape[0] - SUBLANES - ncp, Q_BLOCK), F32)
    ratio = SEL_BLOCK // CMP_STRIDE
    sel = impt_ref[pl.ds(SUBLANES - 1, nsp, stride=ratio), :]
    for o in range(ratio):
        sel = sel + impt_ref[pl.ds(SUBLANES + o, nsp, stride=ratio), :]
    blk = lax.broadcasted_iota(jnp.int32, (nsp, 1), 0)
    blk_f = blk.astype(F32)
    cur = jnp.right_shift(t0 + lax.broadcasted_iota(jnp.int32, (1, Q_BLOCK), 1), 6)
    future = blk > cur
    forced = jnp.logical_or(blk == 0, jnp.logical_or(blk == cur, blk == cur - 1))
    work = jnp.where(forced, -3.0, jnp.where(future, -1.0, sel))
    picked = jnp.where(forced, 1.0, 0.0)
    for _ in range(n_top - 3):
        mx = jnp.max(work, axis=0, keepdims=True)
        first = jnp.min(jnp.where(work == mx, blk_f, float(nsp)), axis=0, keepdims=True)
        hit = blk_f == first
        picked = jnp.where(hit, 1.0, picked)
        work = jnp.where(hit, -3.0, work)
    keep = jnp.where(future, 0.0, picked).astype(BF16)
    eye = (lax.broadcasted_iota(jnp.int32, (Q_BLOCK, Q_BLOCK), 0)
           == lax.broadcasted_iota(jnp.int32, (Q_BLOCK, Q_BLOCK), 1)).astype(BF16)
    keep_q = lax.dot_general(eye, keep, _NT, preferred_element_type=F32)
    bias = jnp.where(keep_q > 0.5, 0.0, NEG_INF)
    bias4 = jnp.concatenate([bias] * GQ, axis=0).astype(BF16)

    for h in range(n_super):
        qaug_ref[h, :, 0:HEAD_DIM] = qb
        qaug_ref[h, :, HEAD_DIM:LANES] = jnp.zeros((rows, LANES - HEAD_DIM), BF16)
        qaug_ref[h, :, LANES:2 * LANES] = bias4[:, h * SEL_SUPER:(h + 1) * SEL_SUPER]

    chunks_per_super = SEL_SUPER * SEL_BLOCK // SEL_CHUNK

    def scores(j):
        start = pl.multiple_of(j * SEL_CHUNK, SEL_CHUNK)
        return lax.dot_general(qaug_ref[j // chunks_per_super], kaug_ref[0, 0, pl.ds(start, SEL_CHUNK), :],
                               _NT, preferred_element_type=F32)

    def accumulate(j, sc, m, acc):
        start = pl.multiple_of(j * SEL_CHUNK, SEL_CHUNK)
        m_new = jnp.maximum(m, jnp.max(sc, axis=-1, keepdims=True))
        p = jnp.exp2(sc - m_new)
        acc = jnp.exp2(m - m_new) * acc + _dot(p.astype(BF16), vs_ref[0, 0, pl.ds(start, SEL_CHUNK), :])
        return m_new, acc

    def flash_pair(i, carry):
        j = 2 * i
        sbuf_ref[1] = scores(j + 1)
        carry = accumulate(j, sbuf_ref[0], *carry)
        sbuf_ref[0] = scores(j + 2)
        return accumulate(j + 1, sbuf_ref[1], *carry)

    def diagonal(j, sc, carry):
        kpos = j * SEL_CHUNK + lax.broadcasted_iota(jnp.int32, (1, SEL_CHUNK), 1)
        return accumulate(j, jnp.where(kpos <= tq4, sc, NEG_INF), *carry)

    def tail_two(carry):
        sbuf_ref[1] = scores(j_diag)
        carry = accumulate(j_diag - 1, sbuf_ref[0], *carry)
        return diagonal(j_diag, sbuf_ref[1], carry)

    def tail_one(carry):
        return diagonal(j_diag, sbuf_ref[0], carry)

    j_diag = t0 // SEL_CHUNK
    n_pairs = j_diag // 2
    carry = (jnp.full((rows, 1), NEG_INF, F32), jnp.zeros((rows, LANES), F32))
    sbuf_ref[0] = scores(0)
    carry = lax.fori_loop(0, n_pairs, flash_pair, carry)
    _, acc_sel = lax.cond(j_diag > 2 * n_pairs, tail_two, tail_one, carry)
    o_sel = acc_sel[:, :HEAD_DIM] / jnp.maximum(acc_sel[:, HEAD_DIM:HEAD_DIM + 1], TINY)

    outs = []
    for r in range(GQ):
        sl = slice(r * Q_BLOCK, (r + 1) * Q_BLOCK)
        outs.append(gts[:, 3 * r:3 * r + 1] * o_cmp[sl]
                    + gts[:, 3 * r + 1:3 * r + 2] * o_sel[sl]
                    + gts[:, 3 * r + 2:3 * r + 3] * o_win[sl])
    o_ref[0] = jnp.concatenate(outs, axis=1)


def _nsa_attention(q, gt, k_cmp, v_cmp, k_aug, v_sel, k_win, v_win):
    b, t, _ = q.shape
    g = NSA_KV_HEADS
    ncp = k_cmp.shape[2]
    nc = t // CMP_STRIDE - CMP_BLOCK // CMP_STRIDE + 1
    ns = t // SEL_BLOCK
    nsp = -(-ns // SEL_SUPER) * SEL_SUPER
    impt_rows = SUBLANES + max(ncp, nsp * (SEL_BLOCK // CMP_STRIDE))
    body = functools.partial(_nsa_body, nc=nc, n_top=min(TOP_N, ns))

    def resident(shape):
        return pl.BlockSpec((1, 1) + shape, lambda bi, gi, c: (gi, bi, 0, 0), pipeline_mode=pl.Buffered(1))

    return pl.pallas_call(
        body,
        grid=(b, g, t // Q_BLOCK),
        in_specs=[
            pl.BlockSpec((1, Q_BLOCK, GQ * HEAD_DIM), lambda bi, gi, c: (bi, c, gi)),
            pl.BlockSpec((1, Q_BLOCK, LANES), lambda bi, gi, c: (bi, c, gi)),
            resident((ncp, HEAD_DIM)),
            resident((ncp, HEAD_DIM)),
            resident((t, 2 * LANES)),
            resident((t, LANES)),
            resident((t, HEAD_DIM)),
            resident((t, LANES)),
        ],
        out_specs=pl.BlockSpec((1, Q_BLOCK, GQ * HEAD_DIM), lambda bi, gi, c: (bi, c, gi)),
        out_shape=jax.ShapeDtypeStruct((b, t, D_NSA), F32),
        scratch_shapes=[pltpu.VMEM((nsp // SEL_SUPER, GQ * Q_BLOCK, 2 * LANES), BF16),
                        pltpu.VMEM((2, GQ * Q_BLOCK, SEL_CHUNK), F32),
                        pltpu.VMEM((impt_rows, Q_BLOCK), F32)],
        compiler_params=_cparams("arbitrary", "arbitrary", "arbitrary"),
        name="nsa_attention",
    )(q, gt, k_cmp, v_cmp, k_aug, v_sel, k_win, v_win)


def _nsa_mixer(q, gt, kc, vc, k_aug, v_aug, k_win, vw_aug, pe_k, pe_v, w1_k, w2_k, w1_v, w2_v):
    b, t, _ = q.shape
    g = NSA_KV_HEADS
    nch = t // CMP_STRIDE

    def chunks(x):
        return x.reshape(g * b, nch, CMP_STRIDE * HEAD_DIM)

    def per_seq(x):
        return x.reshape(g, b, t, x.shape[-1])

    k_cmp, v_cmp = _compress(chunks(kc), chunks(vc), pe_k, pe_v,
                             w1_k.astype(BF16), w2_k.astype(BF16), w1_v.astype(BF16), w2_v.astype(BF16))
    return _nsa_attention(q, gt, k_cmp.reshape(g, b, nch, HEAD_DIM), v_cmp.reshape(g, b, nch, HEAD_DIM),
                          per_seq(k_aug), per_seq(v_aug), per_seq(k_win), per_seq(vw_aug))


def _rwkv_prep_body(p_ref, ph_ref, lo_ref, loh_ref, mu_ref, mul_ref, w0_ref, w2_ref, a0_ref, a2_ref,
                    g2_ref, kk_ref, ka_ref, rk_ref, ones_ref,
                    r_o, w_o, k_o, nkk_o, kka_o, v_o, bv_o, g_o, *, tiles_per_seq):
    tm = p_ref.shape[0]
    first = (pl.program_id(0) % tiles_per_seq) == 0
    row = lax.broadcasted_iota(jnp.int32, (tm, 1), 0)

    def shifted(x_ref, h_ref, m_ref):
        x = x_ref[...]
        last = jnp.where(first, 0.0, h_ref[SUBLANES - 1:SUBLANES, :])
        prev = jnp.where(row == 0, last, pltpu.roll(x, 1, 0))
        return x + (prev - x) * m_ref[...]

    ps = shifted(p_ref, ph_ref, mu_ref)
    lo = shifted(lo_ref, loh_ref, mul_ref)
    r = ps[:, 0:D_RWKV]
    k = ps[:, D_RWKV:2 * D_RWKV]
    v = ps[:, 2 * D_RWKV:3 * D_RWKV]

    z = w0_ref[...] + _dot(jnp.tanh(lo).astype(BF16), w2_ref[...])
    softplus_neg = jnp.maximum(-z, 0.0) + jnp.log(1.0 + jnp.exp(-jnp.abs(z)))
    decay = jnp.exp(-jnp.exp(-softplus_neg - 0.5))
    a = _sigmoid(a0_ref[...] + _dot(lo.astype(BF16), a2_ref[...]))
    gate = _dot(_sigmoid(lo).astype(BF16), g2_ref[...])

    kk = k * kk_ref[...]
    kk = kk / jnp.maximum(jnp.sqrt(_dot3(kk * kk, ones_ref[...])), 1e-12)
    k2 = k * (1.0 + (a - 1.0) * ka_ref[...])
    bonus = _dot3(r * k2 * rk_ref[...], ones_ref[...])

    r_o[...] = r
    w_o[...] = decay
    k_o[...] = k2
    nkk_o[...] = -kk
    kka_o[...] = kk * a
    v_o[...] = v
    bv_o[...] = bonus * v
    g_o[...] = gate


def _rwkv_prep(p_rkv, p_lora, seq, mu, w0, w2p, a0, a2p, g2p, k_k, k_a, r_k, tm=256):
    m = p_rkv.shape[0]
    hb = tm // SUBLANES
    head_ones = (jnp.arange(D_RWKV)[:, None] // RWKV_HEAD_DIM == jnp.arange(D_RWKV)[None, :] // RWKV_HEAD_DIM)
    row_spec = lambda w: pl.BlockSpec((tm, w), lambda i: (i, 0))
    halo_spec = lambda w: pl.BlockSpec((SUBLANES, w), lambda i: (jnp.maximum(i * hb - 1, 0), 0))
    vec = lambda x: x.reshape(1, -1)
    n_lora = p_lora.shape[1]
    oshape = jax.ShapeDtypeStruct((m, D_RWKV), F32)
    return pl.pallas_call(
        functools.partial(_rwkv_prep_body, tiles_per_seq=seq // tm),
        grid=(m // tm,),
        in_specs=[row_spec(3 * D_RWKV), halo_spec(3 * D_RWKV), row_spec(n_lora), halo_spec(n_lora),
                  _const_spec((1, 3 * D_RWKV)), _const_spec((1, n_lora)),
                  _const_spec((1, D_RWKV)), _const_spec((n_lora, D_RWKV)),
                  _const_spec((1, D_RWKV)), _const_spec((n_lora, D_RWKV)),
                  _const_spec((n_lora, D_RWKV)),
                  _const_spec((1, D_RWKV)), _const_spec((1, D_RWKV)), _const_spec((1, D_RWKV)),
                  _const_spec((D_RWKV, D_RWKV))],
        out_specs=[row_spec(D_RWKV)] * 8,
        out_shape=[oshape] * 8,
        compiler_params=_cparams("parallel"),
        name="rwkv_prep",
    )(p_rkv, p_rkv, p_lora, p_lora, vec(mu[:3 * D_RWKV]), vec(mu[3 * D_RWKV:]), vec(w0), w2p,
      vec(a0), a2p, g2p, vec(k_k), vec(k_a), vec(r_k), head_ones.astype(BF16))


def _rwkv_scan_body(r_ref, w_ref, k_ref, nkk_ref, kka_ref, vt_ref, ones_ref, yt_ref, s_ref, vbuf_ref):
    nb = r_ref.shape[0]
    n_sub = vt_ref.shape[1]
    hd = RWKV_HEAD_DIM
    pairs = [(b, hp) for b in range(nb) for hp in range(D_RWKV // LANES)]
    n_rows = len(pairs) * hd
    lo_half = lax.broadcasted_iota(jnp.int32, (n_rows, LANES), 1) < hd

    @pl.when(pl.program_id(0) == 0)
    def _():
        s_ref[...] = jnp.zeros_like(s_ref)

    def half_sums(x):
        sa = jnp.sum(jnp.where(lo_half, x, 0.0), axis=1, keepdims=True)
        sb = jnp.sum(jnp.where(lo_half, 0.0, x), axis=1, keepdims=True)
        return sa, sb

    def sub_chunk(sc, carry):
        base = pl.multiple_of(sc * SCAN_SUB, SCAN_SUB)
        s = s_ref[...]

        def v_tile(tt):
            src = (lax.broadcasted_iota(jnp.int32, (hd, LANES), 1) & hd) + tt
            return jnp.concatenate([jnp.take_along_axis(vt_ref[b, sc, hp], src, axis=1) for b, hp in pairs],
                                   axis=0)

        vbuf_ref[0] = v_tile(0)
        for tt in range(SCAN_SUB):
            tile = pl.ds(pl.multiple_of(base + (tt // SUBLANES) * SUBLANES, SUBLANES), SUBLANES)
            sub = slice(tt % SUBLANES, tt % SUBLANES + 1)

            def rows(ref):
                return jnp.concatenate(
                    [jnp.broadcast_to(ref[b, tile, pl.ds(hp * LANES, LANES)][sub, :], (hd, LANES))
                     for b, hp in pairs], axis=0)

            if tt + 1 < SCAN_SUB:
                vbuf_ref[(tt + 1) % 2] = v_tile(tt + 1)
            sa, sb = half_sums(s * rows(nkk_ref))
            s = (s * rows(w_ref)
                 + jnp.where(lo_half, sa, sb) * rows(kka_ref)
                 + vbuf_ref[tt % 2] * rows(k_ref))
            ys = _dot3(s * rows(r_ref), ones_ref[...])
            for i, (b, hp) in enumerate(pairs):
                for lane in (tt, hd + tt):
                    yt_ref[b, sc, hp, :, lane:lane + 1] = ys[i * hd:(i + 1) * hd, lane:lane + 1]
        s_ref[...] = s
        return carry

    lax.fori_loop(0, n_sub, sub_chunk, 0)


def _rwkv_scan(r, w, k, nkk, kka, v, ct=256):
    b, t, _ = r.shape
    n_sub = ct // SCAN_SUB
    n_pair = D_RWKV // LANES
    hd = RWKV_HEAD_DIM
    vt = v.reshape(b, t // SCAN_SUB, SCAN_SUB, n_pair, 2, hd).transpose(0, 1, 3, 5, 4, 2)
    vt = vt.reshape(b, t // SCAN_SUB, n_pair, hd, LANES)
    row_spec = pl.BlockSpec((b, ct, D_RWKV), lambda i: (0, i, 0))
    t_spec = pl.BlockSpec((b, n_sub, n_pair, hd, LANES), lambda i: (0, i, 0, 0, 0))
    head_ones = (jnp.arange(LANES)[:, None] // hd == jnp.arange(LANES)[None, :] // hd).astype(BF16)
    yt = pl.pallas_call(
        _rwkv_scan_body,
        grid=(t // ct,),
        in_specs=[row_spec] * 5 + [t_spec, _const_spec((LANES, LANES))],
        out_specs=pl.BlockSpec((b, n_sub, n_pair, hd, LANES), lambda i: (0, i, 0, 0, 0)),
        out_shape=jax.ShapeDtypeStruct((b, t // SCAN_SUB, n_pair, hd, LANES), F32),
        scratch_shapes=[pltpu.VMEM((b * n_pair * hd, LANES), F32),
                        pltpu.VMEM((2, b * n_pair * hd, LANES), F32)],
        compiler_params=_cparams("arbitrary"),
        name="rwkv_scan",
    )(r, w, k, nkk, kka, vt, head_ones)
    y = yt.reshape(b, t // SCAN_SUB, n_pair, hd, 2, SCAN_SUB)
    return y.transpose(0, 1, 5, 2, 4, 3).reshape(b, t, D_RWKV)


def _out_proj_body(x_ref, yc_ref, yn_ref, yr_ref, bv_ref, g_ref, lg_ref, lb_ref, avg_ref,
                   wc_ref, wn_ref, wr_ref, o_ref):
    y = yr_ref[...]
    mu = _dot3(y, avg_ref[...])
    dlt = y - mu
    var = _dot3(dlt * dlt, avg_ref[...])
    yr = (dlt * lax.rsqrt(var + RWKV_GN_EPS) * lg_ref[...] + lb_ref[...] + bv_ref[...]) * g_ref[...]
    o_ref[...] = (x_ref[...]
                  + _dot(yc_ref[...].astype(BF16), wc_ref[...])
                  + _dot(yn_ref[...].astype(BF16), wn_ref[...])
                  + _dot(yr.astype(BF16), wr_ref[...]))


def _out_proj(x2d, y_conv, y_nsa, y_scan, bv, gate, ln_g, ln_b, wc, wn, wr, tm=512):
    m, d = x2d.shape
    head_avg = (jnp.arange(D_RWKV)[:, None] // RWKV_HEAD_DIM == jnp.arange(D_RWKV)[None, :] // RWKV_HEAD_DIM)
    head_avg = (head_avg.astype(F32) / RWKV_HEAD_DIM).astype(BF16)
    row_spec = lambda w: pl.BlockSpec((tm, w), lambda i: (i, 0))
    return pl.pallas_call(
        _out_proj_body,
        grid=(m // tm,),
        in_specs=[row_spec(d), row_spec(D_CONV), row_spec(D_NSA), row_spec(D_RWKV), row_spec(D_RWKV),
                  row_spec(D_RWKV), _const_spec((1, D_RWKV)), _const_spec((1, D_RWKV)),
                  _const_spec((D_RWKV, D_RWKV)), _const_spec(wc.shape), _const_spec(wn.shape),
                  _const_spec(wr.shape)],
        out_specs=row_spec(d),
        out_shape=jax.ShapeDtypeStruct((m, d), F32),
        compiler_params=_cparams("parallel"),
        name="out_proj",
    )(x2d, y_conv, y_nsa, y_scan, bv, gate, ln_g.reshape(1, -1), ln_b.reshape(1, -1), head_avg, wc, wn, wr)


def _ffn_body(x_ref, xh_ref, g_ref, wg_ref, wu_ref, cw_ref, cb_ref, wd_ref, o_ref, *, tiles_per_seq):
    tm = x_ref.shape[0]
    x = x_ref[...]
    hb = _rms(x, g_ref[...]).astype(BF16)
    hh = _rms(xh_ref[...], g_ref[...]).astype(BF16)
    a = _dot(hb, wg_ref[...])
    first = (pl.program_id(0) % tiles_per_seq) == 0
    ah = jnp.where(first, 0.0, _dot(hh, wg_ref[...]))
    row = lax.broadcasted_iota(jnp.int32, (tm, 1), 0)
    a1 = jnp.where(row == 0, ah[SUBLANES - 1:SUBLANES], pltpu.roll(a, 1, 0))
    a2 = jnp.where(row == 0, ah[SUBLANES - 2:SUBLANES - 1],
                   jnp.where(row == 1, ah[SUBLANES - 1:SUBLANES], pltpu.roll(a, 2, 0)))
    conv = cw_ref[0:1, :] * a2 + cw_ref[1:2, :] * a1 + cw_ref[2:3, :] * a + cb_ref[...]
    act = conv * _sigmoid(conv) * _dot(hb, wu_ref[...])
    o_ref[...] = x + _dot(act.astype(BF16), wd_ref[...])


def _ffn(x2d, seq, gain, wg, wu, conv_w, conv_b, wd, tm=256):
    m, d = x2d.shape
    f = wg.shape[1]
    hb = tm // SUBLANES
    cw = jnp.zeros((SUBLANES, f), F32).at[:FFN_CONV].set(conv_w)
    return pl.pallas_call(
        functools.partial(_ffn_body, tiles_per_seq=seq // tm),
        grid=(m // tm,),
        in_specs=[pl.BlockSpec((tm, d), lambda i: (i, 0)),
                  pl.BlockSpec((SUBLANES, d), lambda i: (jnp.maximum(i * hb - 1, 0), 0)),
                  _const_spec((1, d)), _const_spec(wg.shape), _const_spec(wu.shape),
                  _const_spec((SUBLANES, f)), _const_spec((1, f)), _const_spec(wd.shape)],
        out_specs=pl.BlockSpec((tm, d), lambda i: (i, 0)),
        out_shape=jax.ShapeDtypeStruct((m, d), F32),
        compiler_params=_cparams("parallel"),
        name="conv_ffn",
    )(x2d, x2d, gain.reshape(1, d), wg, wu, cw, conv_b.reshape(1, f), wd)


def _final_norm_body(x_ref, g_ref, o_ref):
    o_ref[...] = _rms(x_ref[...], g_ref[...])


def _final_norm(x2d, gain, tm=1024):
    m, d = x2d.shape
    return pl.pallas_call(
        _final_norm_body,
        grid=(m // tm,),
        in_specs=[pl.BlockSpec((tm, d), lambda i: (i, 0)), _const_spec((1, d))],
        out_specs=pl.BlockSpec((tm, d), lambda i: (i, 0)),
        out_shape=jax.ShapeDtypeStruct((m, d), F32),
        compiler_params=_cparams("parallel"),
        name="final_norm",
    )(x2d, gain.reshape(1, d))


def _split_w_in(w):
    o = 0
    wc = w[:, o:o + CONV_COLS]; o += CONV_COLS
    wq = w[:, o:o + D_NSA]; o += D_NSA
    wkv = w[:, o:o + 6 * KV_COLS]; o += 6 * KV_COLS
    wgt = w[:, o:o + 3 * NSA_HEADS]; o += 3 * NSA_HEADS
    wrkv = w[:, o:o + 3 * D_RWKV]; o += 3 * D_RWKV
    wlora = w[:, o:]
    per_group = 3 * GQ
    wgt_pad = jnp.zeros((w.shape[0], NSA_KV_HEADS * LANES), w.dtype)
    for g in range(NSA_KV_HEADS):
        wgt_pad = wgt_pad.at[:, g * LANES:g * LANES + per_group].set(wgt[:, g * per_group:(g + 1) * per_group])
    flat = [z.astype(BF16) for z in (wc, wq, wgt_pad, wrkv, wlora)]

    def per_group_cols(i, pad):
        wi = wkv[:, i * KV_COLS:(i + 1) * KV_COLS].reshape(w.shape[0], NSA_KV_HEADS, HEAD_DIM).transpose(1, 0, 2)
        if pad:
            wi = jnp.pad(wi, ((0, 0), (0, 0), (0, LANES - HEAD_DIM)))
        return wi.astype(BF16)

    kv = [per_group_cols(i, pad) for i, pad in enumerate((False, False, True, True, False, True))]
    return flat, kv


def _pad_lora(w, row0, n_rows):
    return jnp.zeros((n_rows, w.shape[1]), F32).at[row0:row0 + w.shape[0]].set(w).astype(BF16)


def kernel(x, w_in, w_out, norm_mix, norm_ffn, norm_final, conv_dw_w, conv_dw_b, conv_ln_g, conv_ln_b, cmp_pe_k, cmp_pe_v, cmp_w1_k, cmp_w2_k, cmp_w1_v, cmp_w2_v, rwkv_mu, rwkv_w0, rwkv_w2, rwkv_a0, rwkv_a2, rwkv_g2, rwkv_k_k, rwkv_k_a, rwkv_r_k, rwkv_ln_g, rwkv_ln_b, ffn_w_gate, ffn_w_up, ffn_conv_w, ffn_conv_b, ffn_w_down):
    b, t, d = x.shape
    m = b * t
    n_lora = DECAY_LORA + AAA_LORA + GATE_LORA
    x2 = x.reshape(m, d)
    for i in range(w_in.shape[0]):
        p_conv, q, gt, p_rkv, p_lora, *kv_ops = _norm_proj(x2, t, norm_mix[i], *_split_w_in(w_in[i]))
        y_conv = _conv_mixer(p_conv.reshape(b, t, -1), conv_dw_w[i], conv_dw_b[i], conv_ln_g[i], conv_ln_b[i])
        y_nsa = _nsa_mixer(q.reshape(b, t, -1), gt.reshape(b, t, -1), *kv_ops,
                           cmp_pe_k[i], cmp_pe_v[i], cmp_w1_k[i], cmp_w2_k[i], cmp_w1_v[i], cmp_w2_v[i])
        r, w, k2, nkk, kka, v, bv, gate = _rwkv_prep(
            p_rkv, p_lora, t, rwkv_mu[i], rwkv_w0[i],
            _pad_lora(rwkv_w2[i], 0, n_lora), rwkv_a0[i],
            _pad_lora(rwkv_a2[i], DECAY_LORA, n_lora),
            _pad_lora(rwkv_g2[i], DECAY_LORA + AAA_LORA, n_lora),
            rwkv_k_k[i], rwkv_k_a[i], rwkv_r_k[i].reshape(-1))
        s3 = lambda z: z.reshape(b, t, D_RWKV)
        y_scan = _rwkv_scan(s3(r), s3(w), s3(k2), s3(nkk), s3(kka), s3(v))
        wo = w_out[i].astype(BF16)
        x2 = _out_proj(x2, y_conv.reshape(m, -1), y_nsa.reshape(m, -1), y_scan.reshape(m, -1), bv, gate,
                       rwkv_ln_g[i], rwkv_ln_b[i], wo[:D_CONV], wo[D_CONV:D_CONV + D_NSA], wo[D_CONV + D_NSA:])
        x2 = _ffn(x2, t, norm_ffn[i], ffn_w_gate[i].astype(BF16), ffn_w_up[i].astype(BF16),
                  ffn_conv_w[i], ffn_conv_b[i], ffn_w_down[i].astype(BF16))
    return _final_norm(x2, norm_final).reshape(b, t, d)
```

```python
import functools

import jax
import jax.numpy as jnp
from jax import lax
from jax.experimental import pallas as pl
from jax.experimental.pallas import tpu as pltpu

F32 = jnp.float32
BF16 = jnp.bfloat16

D_MODEL = 1024
DEPTH = 4
D_CONV = 256
CONV_KERNEL = 31
NSA_HEADS = 8
NSA_KV_HEADS = 2
HEAD_DIM = 64
GQ = NSA_HEADS // NSA_KV_HEADS
D_NSA = NSA_HEADS * HEAD_DIM
RWKV_HEADS = 4
RWKV_HEAD_DIM = 64
D_RWKV = RWKV_HEADS * RWKV_HEAD_DIM
CMP_BLOCK = 32
CMP_STRIDE = 16
SEL_BLOCK = 64
TOP_N = 16
WINDOW = 512
Q_BLOCK = 128
FORCE_BONUS = 1.0e4
DECAY_LORA = 32
AAA_LORA = 32
GATE_LORA = 64
RWKV_GN_EPS = 64e-5
D_FF = 2816
FFN_CONV = 3
RMS_EPS = 1e-6
LN_EPS = 1e-5
NEG_INF = -1e30
TINY = 1e-30
LOG2_E = 1.4426950408889634

CONV_COLS = 2 * D_CONV
KV_COLS = NSA_KV_HEADS * HEAD_DIM
NSA_COLS = D_NSA + 6 * KV_COLS + 3 * NSA_HEADS
RWKV_COLS = 3 * D_RWKV + DECAY_LORA + AAA_LORA + GATE_LORA

LANES = 128
SUBLANES = 8
VMEM_LIMIT_BYTES = 56 * 1024 * 1024

SEL_SUPER = LANES
SEL_CHUNK = 1024
WIN_KEYS = WINDOW + Q_BLOCK
SCAN_SUB = RWKV_HEAD_DIM
CONV_HALO = 32

_NT = (((1,), (1,)), ((), ()))


def _cparams(*sem):
    return pltpu.CompilerParams(dimension_semantics=sem, vmem_limit_bytes=VMEM_LIMIT_BYTES)


def _const_spec(shape):
    nd = len(shape)
    return pl.BlockSpec(shape, lambda *_: (0,) * nd, pipeline_mode=pl.Buffered(1))


def _sigmoid(x):
    return 1.0 / (1.0 + jnp.exp(-x))


def _dot(a, b):
    return jnp.dot(a, b, preferred_element_type=F32)


def _dot3(a, b_exact):
    a1 = a.astype(BF16)
    r1 = a - a1.astype(F32)
    a2 = r1.astype(BF16)
    a3 = (r1 - a2.astype(F32)).astype(BF16)
    return _dot(a1, b_exact) + _dot(a2, b_exact) + _dot(a3, b_exact)


def _rms(x, g):
    return x * lax.rsqrt(jnp.mean(x * x, axis=-1, keepdims=True) + RMS_EPS) * g


def _norm_proj_body(x_ref, g_ref, *refs, n_flat, tiles_per_seq):
    tm = x_ref.shape[0]
    hb = _rms(x_ref[...], g_ref[...]).astype(BF16)
    w_flat, (wkc, wvc, wks, wvs, wkw, wvw) = refs[:n_flat], refs[n_flat:n_flat + 6]
    o_flat, (okc, ovc, okaug, ovaug, okw, ovwaug) = refs[n_flat + 6:2 * n_flat + 6], refs[2 * n_flat + 6:]
    for w_ref, o_ref in zip(w_flat, o_flat):
        o_ref[...] = _dot(hb, w_ref[...])
    pos = (pl.program_id(0) % tiles_per_seq) * tm + lax.broadcasted_iota(jnp.int32, (tm, 1), 0)
    lane = lax.broadcasted_iota(jnp.int32, (1, LANES), 1)
    onehot = (lane == (jnp.right_shift(pos, 6) & (SEL_SUPER - 1))).astype(BF16)
    ones_col = (lane == HEAD_DIM).astype(F32)
    for g in range(NSA_KV_HEADS):
        okc[g] = _dot(hb, wkc[g])
        ovc[g] = _dot(hb, wvc[g])
        okaug[g] = jnp.concatenate([_dot(hb, wks[g]).astype(BF16), onehot], axis=1)
        ovaug[g] = (_dot(hb, wvs[g]) + ones_col).astype(BF16)
        okw[g] = _dot(hb, wkw[g]).astype(BF16)
        ovwaug[g] = (_dot(hb, wvw[g]) + ones_col).astype(BF16)


def _norm_proj(x2d, seq, gain, w_flat, w_kv, tm=512):
    m, d = x2d.shape
    g = NSA_KV_HEADS
    in_specs = [pl.BlockSpec((tm, d), lambda i: (i, 0)), _const_spec((1, d))]
    in_specs += [_const_spec(w.shape) for w in w_flat + w_kv]
    kv_out = [(HEAD_DIM, F32), (HEAD_DIM, F32), (2 * LANES, BF16), (LANES, BF16), (HEAD_DIM, BF16), (LANES, BF16)]
    out_specs = [pl.BlockSpec((tm, w.shape[1]), lambda i: (i, 0)) for w in w_flat]
    out_specs += [pl.BlockSpec((g, tm, width), lambda i: (0, i, 0)) for width, _ in kv_out]
    out_shape = [jax.ShapeDtypeStruct((m, w.shape[1]), F32) for w in w_flat]
    out_shape += [jax.ShapeDtypeStruct((g, m, width), dt) for width, dt in kv_out]
    return pl.pallas_call(
        functools.partial(_norm_proj_body, n_flat=len(w_flat), tiles_per_seq=seq // tm),
        grid=(m // tm,),
        in_specs=in_specs,
        out_specs=out_specs,
        out_shape=out_shape,
        compiler_params=_cparams("parallel"),
        name="norm_proj",
    )(x2d, gain.reshape(1, d), *w_flat, *w_kv)


def _conv_body(p_ref, h_ref, w_ref, b_ref, lg_ref, lb_ref, o_ref):
    tc = p_ref.shape[1]
    ext = jnp.concatenate([h_ref[0], p_ref[0]], axis=0)
    y = ext[:, :D_CONV] * _sigmoid(ext[:, D_CONV:])
    row = lax.broadcasted_iota(jnp.int32, (CONV_HALO + tc, 1), 0)
    before_start = jnp.logical_and(pl.program_id(1) == 0, row < CONV_HALO)
    y = jnp.where(before_start, 0.0, y)
    acc = jnp.zeros((tc, D_CONV), F32) + b_ref[...]
    for j in range(CONV_KERNEL):
        s = CONV_KERNEL - 1 - j
        ys = y if s == 0 else pltpu.roll(y, s, 0)
        acc = acc + w_ref[j:j + 1, :] * ys[CONV_HALO:, :]
    mu = jnp.mean(acc, axis=-1, keepdims=True)
    dlt = acc - mu
    var = jnp.mean(dlt * dlt, axis=-1, keepdims=True)
    z = dlt * lax.rsqrt(var + LN_EPS) * lg_ref[...] + lb_ref[...]
    o_ref[0] = z * _sigmoid(z)


def _conv_mixer(p_conv, dw_w, dw_b, ln_g, ln_b, tc=512):
    b, t, _ = p_conv.shape
    hb = tc // CONV_HALO
    wpad = jnp.zeros((CONV_HALO, D_CONV), F32).at[:CONV_KERNEL].set(dw_w)
    return pl.pallas_call(
        _conv_body,
        grid=(b, t // tc),
        in_specs=[
            pl.BlockSpec((1, tc, CONV_COLS), lambda bi, i: (bi, i, 0)),
            pl.BlockSpec((1, CONV_HALO, CONV_COLS), lambda bi, i: (bi, jnp.maximum(i * hb - 1, 0), 0)),
            _const_spec((CONV_HALO, D_CONV)),
            _const_spec((1, D_CONV)),
            _const_spec((1, D_CONV)),
            _const_spec((1, D_CONV)),
        ],
        out_specs=pl.BlockSpec((1, tc, D_CONV), lambda bi, i: (bi, i, 0)),
        out_shape=jax.ShapeDtypeStruct((b, t, D_CONV), F32),
        compiler_params=_cparams("parallel", "parallel"),
        name="conv_mixer",
    )(p_conv, p_conv, wpad, dw_b.reshape(1, -1), ln_g.reshape(1, -1), ln_b.reshape(1, -1))


def _gelu_tanh(x):
    return 0.5 * x * (1.0 + jnp.tanh(0.7978845608028654 * (x + 0.044715 * (x * x * x))))


def _compress_body(xk_ref, xv_ref, pek_ref, pev_ref, w1k_ref, w2k_ref, w1v_ref, w2v_ref, ok_ref, ov_ref):
    half = CMP_STRIDE * HEAD_DIM
    nch = xk_ref.shape[1]

    def one(x_ref, pe_ref, w1_ref, w2_ref, o_ref):
        x = x_ref[0]
        a = _dot((x + pe_ref[:, :half]).astype(BF16), w1_ref[:half, :])
        bb = _dot((x + pe_ref[:, half:]).astype(BF16), w1_ref[half:, :])
        pre = a + pltpu.roll(bb, nch - 1, 0)
        o_ref[0] = _dot(_gelu_tanh(pre).astype(BF16), w2_ref[...]).astype(BF16)

    one(xk_ref, pek_ref, w1k_ref, w2k_ref, ok_ref)
    one(xv_ref, pev_ref, w1v_ref, w2v_ref, ov_ref)


def _compress(xk, xv, pe_k, pe_v, w1_k, w2_k, w1_v, w2_v):
    bg, nch, width = xk.shape
    xspec = pl.BlockSpec((1, nch, width), lambda i: (i, 0, 0))
    ospec = pl.BlockSpec((1, nch, HEAD_DIM), lambda i: (i, 0, 0))
    oshape = jax.ShapeDtypeStruct((bg, nch, HEAD_DIM), BF16)
    return pl.pallas_call(
        _compress_body,
        grid=(bg,),
        in_specs=[xspec, xspec, _const_spec((1, 2 * width)), _const_spec((1, 2 * width)),
                  _const_spec(w1_k.shape), _const_spec(w2_k.shape),
                  _const_spec(w1_v.shape), _const_spec(w2_v.shape)],
        out_specs=[ospec, ospec],
        out_shape=[oshape, oshape],
        compiler_params=_cparams("parallel"),
        name="nsa_compress",
    )(xk, xv, pe_k.reshape(1, -1), pe_v.reshape(1, -1), w1_k, w2_k, w1_v, w2_v)


def _softmax_rows(s2, mask):
    s2 = jnp.where(mask, s2, NEG_INF)
    m = jnp.max(s2, axis=-1, keepdims=True)
    e = jnp.where(mask, jnp.exp2(s2 - m), 0.0)
    return e / jnp.maximum(jnp.sum(e, axis=-1, keepdims=True), TINY)


def _nsa_body(q_ref, gt_ref, kc_ref, vc_ref, kaug_ref, vs_ref, kw_ref, vw_ref,
              o_ref, qaug_ref, sbuf_ref, impt_ref, *, nc, n_top):
    ncp = kc_ref.shape[2]
    n_super = qaug_ref.shape[0]
    nsp = n_super * SEL_SUPER
    rows = GQ * Q_BLOCK
    t0 = pl.program_id(2) * Q_BLOCK

    qf = q_ref[0] * (HEAD_DIM ** -0.5 * LOG2_E)
    q4 = jnp.concatenate([qf[:, r * HEAD_DIM:(r + 1) * HEAD_DIM] for r in range(GQ)], axis=0)
    qb = q4.astype(BF16)
    tq4 = t0 + (lax.broadcasted_iota(jnp.int32, (rows, 1), 0) & (Q_BLOCK - 1))

    s = lax.dot_general(qb, kc_ref[0, 0], _NT, preferred_element_type=F32)
    n_idx = lax.broadcasted_iota(jnp.int32, (1, ncp), 1)
    cmp_end = jnp.where(n_idx < nc, n_idx * CMP_STRIDE + (CMP_BLOCK - 1), jnp.iinfo(jnp.int32).max)
    pc = _softmax_rows(s, cmp_end <= tq4)
    o_cmp = _dot(pc.astype(BF16), vc_ref[0, 0])

    ws = pl.multiple_of(jnp.maximum(t0 - WINDOW, 0), Q_BLOCK)
    sw = lax.dot_general(qb, kw_ref[0, 0, pl.ds(ws, WIN_KEYS), :], _NT, preferred_element_type=F32)
    diff = tq4 - (ws + lax.broadcasted_iota(jnp.int32, (1, WIN_KEYS), 1))
    in_window = lax.bitcast_convert_type(diff, jnp.uint32) < jnp.uint32(WINDOW)
    sw = jnp.where(in_window, sw, NEG_INF)
    ew = jnp.exp2(sw - jnp.max(sw, axis=-1, keepdims=True))
    aw = _dot(ew.astype(BF16), vw_ref[0, 0, pl.ds(ws, WIN_KEYS), :])
    o_win = aw[:, :HEAD_DIM] / jnp.maximum(aw[:, HEAD_DIM:HEAD_DIM + 1], TINY)
    gts = _sigmoid(gt_ref[0])

    imp = pc[0:Q_BLOCK]
    for r in range(1, GQ):
        imp = imp + pc[r * Q_BLOCK:(r + 1) * Q_BLOCK]
    impt_ref[0:SUBLANES, :] = jnp.zeros((SUBLANES, Q_BLOCK), F32)
    impt_ref[SUBLANES:SUBLANES + ncp, :] = imp.T
    if impt_ref.shape[0] > SUBLANES + ncp:
        impt_ref[SUBLANES + ncp:, :] = jnp.zeros((impt_ref.shape[0] - SUBLANES - ncp, Q_BLOCK), F32)
    ratio = SEL_BLOCK // CMP_STRIDE
    sel = impt_ref[pl.ds(SUBLANES - 1, nsp, stride=ratio), :]
    for o in range(ratio):
        sel = sel + impt_ref[pl.ds(SUBLANES + o, nsp, stride=ratio), :]
    blk = lax.broadcasted_iota(jnp.int32, (nsp, 1), 0)
    blk_f = blk.astype(F32)
    cur = jnp.right_shift(t0 + lax.broadcasted_iota(jnp.int32, (1, Q_BLOCK), 1), 6)
    future = blk > cur
    forced = jnp.logical_or(blk == 0, jnp.logical_or(blk == cur, blk == cur - 1))
    work = jnp.where(forced, -3.0, jnp.where(future, -1.0, sel))
    picked = jnp.where(forced, 1.0, 0.0)
    for _ in range(n_top - 3):
        mx = jnp.max(work, axis=0, keepdims=True)
        first = jnp.min(jnp.where(work == mx, blk_f, float(nsp)), axis=0, keepdims=True)
        hit = blk_f == first
        picked = jnp.where(hit, 1.0, picked)
        work = jnp.where(hit, -3.0, work)
    keep = jnp.where(future, 0.0, picked).astype(BF16)
    eye = (lax.broadcasted_iota(jnp.int32, (Q_BLOCK, Q_BLOCK), 0)
           == lax.broadcasted_iota(jnp.int32, (Q_BLOCK, Q_BLOCK), 1)).astype(BF16)
    keep_q = lax.dot_general(eye, keep, _NT, preferred_element_type=F32)
    bias = jnp.where(keep_q > 0.5, 0.0, NEG_INF)
    bias4 = jnp.concatenate([bias] * GQ, axis=0).astype(BF16)

    for h in range(n_super):
        qaug_ref[h, :, 0:HEAD_DIM] = qb
        qaug_ref[h, :, HEAD_DIM:LANES] = jnp.zeros((rows, LANES - HEAD_DIM), BF16)
        qaug_ref[h, :, LANES:2 * LANES] = bias4[:, h * SEL_SUPER:(h + 1) * SEL_SUPER]

    chunks_per_super = SEL_SUPER * SEL_BLOCK // SEL_CHUNK

    def scores(j):
        start = pl.multiple_of(j * SEL_CHUNK, SEL_CHUNK)
        return lax.dot_general(qaug_ref[j // chunks_per_super], kaug_ref[0, 0, pl.ds(start, SEL_CHUNK), :],
                               _NT, preferred_element_type=F32)

    def accumulate(j, sc, m, acc):
        start = pl.multiple_of(j * SEL_CHUNK, SEL_CHUNK)
        m_new = jnp.maximum(m, jnp.max(sc, axis=-1, keepdims=True))
        p = jnp.exp2(sc - m_new)
        acc = jnp.exp2(m - m_new) * acc + _dot(p.astype(BF16), vs_ref[0, 0, pl.ds(start, SEL_CHUNK), :])
        return m_new, acc

    def flash_pair(i, carry):
        j = 2 * i
        sbuf_ref[1] = scores(j + 1)
        carry = accumulate(j, sbuf_ref[0], *carry)
        sbuf_ref[0] = scores(j + 2)
        return accumulate(j + 1, sbuf_ref[1], *carry)

    def diagonal(j, sc, carry):
        kpos = j * SEL_CHUNK + lax.broadcasted_iota(jnp.int32, (1, SEL_CHUNK), 1)
        return accumulate(j, jnp.where(kpos <= tq4, sc, NEG_INF), *carry)

    def tail_two(carry):
        sbuf_ref[1] = scores(j_diag)
        carry = accumulate(j_diag - 1, sbuf_ref[0], *carry)
        return diagonal(j_diag, sbuf_ref[1], carry)

    def tail_one(carry):
        return diagonal(j_diag, sbuf_ref[0], carry)

    j_diag = t0 // SEL_CHUNK
    n_pairs = j_diag // 2
    carry = (jnp.full((rows, 1), NEG_INF, F32), jnp.zeros((rows, LANES), F32))
    sbuf_ref[0] = scores(0)
    carry = lax.fori_loop(0, n_pairs, flash_pair, carry)
    _, acc_sel = lax.cond(j_diag > 2 * n_pairs, tail_two, tail_one, carry)
    o_sel = acc_sel[:, :HEAD_DIM] / jnp.maximum(acc_sel[:, HEAD_DIM:HEAD_DIM + 1], TINY)

    outs = []
    for r in range(GQ):
        sl = slice(r * Q_BLOCK, (r + 1) * Q_BLOCK)
        outs.append(gts[:, 3 * r:3 * r + 1] * o_cmp[sl]
                    + gts[:, 3 * r + 1:3 * r + 2] * o_sel[sl]
                    + gts[:, 3 * r + 2:3 * r + 3] * o_win[sl])
    o_ref[0] = jnp.concatenate(outs, axis=1)


def _nsa_attention(q, gt, k_cmp, v_cmp, k_aug, v_sel, k_win, v_win):
    b, t, _ = q.shape
    g = NSA_KV_HEADS
    ncp = k_cmp.shape[2]
    nc = t // CMP_STRIDE - CMP_BLOCK // CMP_STRIDE + 1
    ns = t // SEL_BLOCK
    nsp = -(-ns // SEL_SUPER) * SEL_SUPER
    impt_rows = SUBLANES + max(ncp, nsp * (SEL_BLOCK // CMP_STRIDE))
    body = functools.partial(_nsa_body, nc=nc, n_top=min(TOP_N, ns))

    def resident(shape):
        return pl.BlockSpec((1, 1) + shape, lambda bi, gi, c: (gi, bi, 0, 0), pipeline_mode=pl.Buffered(1))

    return pl.pallas_call(
        body,
        grid=(b, g, t // Q_BLOCK),
        in_specs=[
            pl.BlockSpec((1, Q_BLOCK, GQ * HEAD_DIM), lambda bi, gi, c: (bi, c, gi)),
            pl.BlockSpec((1, Q_BLOCK, LANES), lambda bi, gi, c: (bi, c, gi)),
            resident((ncp, HEAD_DIM)),
            resident((ncp, HEAD_DIM)),
            resident((t, 2 * LANES)),
            resident((t, LANES)),
            resident((t, HEAD_DIM)),
            resident((t, LANES)),
        ],
        out_specs=pl.BlockSpec((1, Q_BLOCK, GQ * HEAD_DIM), lambda bi, gi, c: (bi, c, gi)),
        out_shape=jax.ShapeDtypeStruct((b, t, D_NSA), F32),
        scratch_shapes=[pltpu.VMEM((nsp // SEL_SUPER, GQ * Q_BLOCK, 2 * LANES), BF16),
                        pltpu.VMEM((2, GQ * Q_BLOCK, SEL_CHUNK), F32),
                        pltpu.VMEM((impt_rows, Q_BLOCK), F32)],
        compiler_params=_cparams("arbitrary", "arbitrary", "arbitrary"),
        name="nsa_attention",
    )(q, gt, k_cmp, v_cmp, k_aug, v_sel, k_win, v_win)


def _nsa_mixer(q, gt, kc, vc, k_aug, v_aug, k_win, vw_aug, pe_k, pe_v, w1_k, w2_k, w1_v, w2_v):
    b, t, _ = q.shape
    g = NSA_KV_HEADS
    nch = t // CMP_STRIDE

    def chunks(x):
        return x.reshape(g * b, nch, CMP_STRIDE * HEAD_DIM)

    def per_seq(x):
        return x.reshape(g, b, t, x.shape[-1])

    k_cmp, v_cmp = _compress(chunks(kc), chunks(vc), pe_k, pe_v,
                             w1_k.astype(BF16), w2_k.astype(BF16), w1_v.astype(BF16), w2_v.astype(BF16))
    return _nsa_attention(q, gt, k_cmp.reshape(g, b, nch, HEAD_DIM), v_cmp.reshape(g, b, nch, HEAD_DIM),
                          per_seq(k_aug), per_seq(v_aug), per_seq(k_win), per_seq(vw_aug))


def _rwkv_prep_body(p_ref, ph_ref, lo_ref, loh_ref, mu_ref, mul_ref, w0_ref, w2_ref, a0_ref, a2_ref,
                    g2_ref, kk_ref, ka_ref, rk_ref, ones_ref,
                    r_o, w_o, k_o, nkk_o, kka_o, v_o, bv_o, g_o, *, tiles_per_seq):
    tm = p_ref.shape[0]
    first = (pl.program_id(0) % tiles_per_seq) == 0
    row = lax.broadcasted_iota(jnp.int32, (tm, 1), 0)

    def shifted(x_ref, h_ref, m_ref):
        x = x_ref[...]
        last = jnp.where(first, 0.0, h_ref[SUBLANES - 1:SUBLANES, :])
        prev = jnp.where(row == 0, last, pltpu.roll(x, 1, 0))
        return x + (prev - x) * m_ref[...]

    ps = shifted(p_ref, ph_ref, mu_ref)
    lo = shifted(lo_ref, loh_ref, mul_ref)
    r = ps[:, 0:D_RWKV]
    k = ps[:, D_RWKV:2 * D_RWKV]
    v = ps[:, 2 * D_RWKV:3 * D_RWKV]

    z = w0_ref[...] + _dot(jnp.tanh(lo).astype(BF16), w2_ref[...])
    softplus_neg = jnp.maximum(-z, 0.0) + jnp.log(1.0 + jnp.exp(-jnp.abs(z)))
    decay = jnp.exp(-jnp.exp(-softplus_neg - 0.5))
    a = _sigmoid(a0_ref[...] + _dot(lo.astype(BF16), a2_ref[...]))
    gate = _dot(_sigmoid(lo).astype(BF16), g2_ref[...])

    kk = k * kk_ref[...]
    kk = kk / jnp.maximum(jnp.sqrt(_dot3(kk * kk, ones_ref[...])), 1e-12)
    k2 = k * (1.0 + (a - 1.0) * ka_ref[...])
    bonus = _dot3(r * k2 * rk_ref[...], ones_ref[...])

    r_o[...] = r
    w_o[...] = decay
    k_o[...] = k2
    nkk_o[...] = -kk
    kka_o[...] = kk * a
    v_o[...] = v
    bv_o[...] = bonus * v
    g_o[...] = gate


def _rwkv_prep(p_rkv, p_lora, seq, mu, w0, w2p, a0, a2p, g2p, k_k, k_a, r_k, tm=256):
    m = p_rkv.shape[0]
    hb = tm // SUBLANES
    head_ones = (jnp.arange(D_RWKV)[:, None] // RWKV_HEAD_DIM == jnp.arange(D_RWKV)[None, :] // RWKV_HEAD_DIM)
    row_spec = lambda w: pl.BlockSpec((tm, w), lambda i: (i, 0))
    halo_spec = lambda w: pl.BlockSpec((SUBLANES, w), lambda i: (jnp.maximum(i * hb - 1, 0), 0))
    vec = lambda x: x.reshape(1, -1)
    n_lora = p_lora.shape[1]
    oshape = jax.ShapeDtypeStruct((m, D_RWKV), F32)
    return pl.pallas_call(
        functools.partial(_rwkv_prep_body, tiles_per_seq=seq // tm),
        grid=(m // tm,),
        in_specs=[row_spec(3 * D_RWKV), halo_spec(3 * D_RWKV), row_spec(n_lora), halo_spec(n_lora),
                  _const_spec((1, 3 * D_RWKV)), _const_spec((1, n_lora)),
                  _const_spec((1, D_RWKV)), _const_spec((n_lora, D_RWKV)),
                  _const_spec((1, D_RWKV)), _const_spec((n_lora, D_RWKV)),
                  _const_spec((n_lora, D_RWKV)),
                  _const_spec((1, D_RWKV)), _const_spec((1, D_RWKV)), _const_spec((1, D_RWKV)),
                  _const_spec((D_RWKV, D_RWKV))],
        out_specs=[row_spec(D_RWKV)] * 8,
        out_shape=[oshape] * 8,
        compiler_params=_cparams("parallel"),
        name="rwkv_prep",
    )(p_rkv, p_rkv, p_lora, p_lora, vec(mu[:3 * D_RWKV]), vec(mu[3 * D_RWKV:]), vec(w0), w2p,
      vec(a0), a2p, g2p, vec(k_k), vec(k_a), vec(r_k), head_ones.astype(BF16))


def _rwkv_scan_body(r_ref, w_ref, k_ref, nkk_ref, kka_ref, vt_ref, ones_ref, yt_ref, s_ref, vbuf_ref):
    nb = r_ref.shape[0]
    n_sub = vt_ref.shape[1]
    hd = RWKV_HEAD_DIM
    pairs = [(b, hp) for b in range(nb) for hp in range(D_RWKV // LANES)]
    n_rows = len(pairs) * hd
    lo_half = lax.broadcasted_iota(jnp.int32, (n_rows, LANES), 1) < hd

    @pl.when(pl.program_id(0) == 0)
    def _():
        s_ref[...] = jnp.zeros_like(s_ref)

    def half_sums(x):
        sa = jnp.sum(jnp.where(lo_half, x, 0.0), axis=1, keepdims=True)
        sb = jnp.sum(jnp.where(lo_half, 0.0, x), axis=1, keepdims=True)
        return sa, sb

    def sub_chunk(sc, carry):
        base = pl.multiple_of(sc * SCAN_SUB, SCAN_SUB)
        s = s_ref[...]

        def v_tile(tt):
            src = (lax.broadcasted_iota(jnp.int32, (hd, LANES), 1) & hd) + tt
            return jnp.concatenate([jnp.take_along_axis(vt_ref[b, sc, hp], src, axis=1) for b, hp in pairs],
                                   axis=0)

        vbuf_ref[0] = v_tile(0)
        for tt in range(SCAN_SUB):
            tile = pl.ds(pl.multiple_of(base + (tt // SUBLANES) * SUBLANES, SUBLANES), SUBLANES)
            sub = slice(tt % SUBLANES, tt % SUBLANES + 1)

            def rows(ref):
                return jnp.concatenate(
                    [jnp.broadcast_to(ref[b, tile, pl.ds(hp * LANES, LANES)][sub, :], (hd, LANES))
                     for b, hp in pairs], axis=0)

            if tt + 1 < SCAN_SUB:
                vbuf_ref[(tt + 1) % 2] = v_tile(tt + 1)
            sa, sb = half_sums(s * rows(nkk_ref))
            s = (s * rows(w_ref)
                 + jnp.where(lo_half, sa, sb) * rows(kka_ref)
                 + vbuf_ref[tt % 2] * rows(k_ref))
            ys = _dot3(s * rows(r_ref), ones_ref[...])
            for i, (b, hp) in enumerate(pairs):
                for lane in (tt, hd + tt):
                    yt_ref[b, sc, hp, :, lane:lane + 1] = ys[i * hd:(i + 1) * hd, lane:lane + 1]
        s_ref[...] = s
        return carry

    lax.fori_loop(0, n_sub, sub_chunk, 0)


def _rwkv_scan(r, w, k, nkk, kka, v, ct=256):
    b, t, _ = r.shape
    n_sub = ct // SCAN_SUB
    n_pair = D_RWKV // LANES
    hd = RWKV_HEAD_DIM
    vt = v.reshape(b, t // SCAN_SUB, SCAN_SUB, n_pair, 2, hd).transpose(0, 1, 3, 5, 4, 2)
    vt = vt.reshape(b, t // SCAN_SUB, n_pair, hd, LANES)
    row_spec = pl.BlockSpec((b, ct, D_RWKV), lambda i: (0, i, 0))
    t_spec = pl.BlockSpec((b, n_sub, n_pair, hd, LANES), lambda i: (0, i, 0, 0, 0))
    head_ones = (jnp.arange(LANES)[:, None] // hd == jnp.arange(LANES)[None, :] // hd).astype(BF16)
    yt = pl.pallas_call(
        _rwkv_scan_body,
        grid=(t // ct,),
        in_specs=[row_spec] * 5 + [t_spec, _const_spec((LANES, LANES))],
        out_specs=pl.BlockSpec((b, n_sub, n_pair, hd, LANES), lambda i: (0, i, 0, 0, 0)),
        out_shape=jax.ShapeDtypeStruct((b, t // SCAN_SUB, n_pair, hd, LANES), F32),
        scratch_shapes=[pltpu.VMEM((b * n_pair * hd, LANES), F32),
                        pltpu.VMEM((2, b * n_pair * hd, LANES), F32)],
        compiler_params=_cparams("arbitrary"),
        name="rwkv_scan",
    )(r, w, k, nkk, kka, vt, head_ones)
    y = yt.reshape(b, t // SCAN_SUB, n_pair, hd, 2, SCAN_SUB)
    return y.transpose(0, 1, 5, 2, 4, 3).reshape(b, t, D_RWKV)


def _out_proj_body(x_ref, yc_ref, yn_ref, yr_ref, bv_ref, g_ref, lg_ref, lb_ref, avg_ref,
                   wc_ref, wn_ref, wr_ref, o_ref):
    y = yr_ref[...]
    mu = _dot3(y, avg_ref[...])
    dlt = y - mu
    var = _dot3(dlt * dlt, avg_ref[...])
    yr = (dlt * lax.rsqrt(var + RWKV_GN_EPS) * lg_ref[...] + lb_ref[...] + bv_ref[...]) * g_ref[...]
    o_ref[...] = (x_ref[...]
                  + _dot(yc_ref[...].astype(BF16), wc_ref[...])
                  + _dot(yn_ref[...].astype(BF16), wn_ref[...])
                  + _dot(yr.astype(BF16), wr_ref[...]))


def _out_proj(x2d, y_conv, y_nsa, y_scan, bv, gate, ln_g, ln_b, wc, wn, wr, tm=512):
    m, d = x2d.shape
    head_avg = (jnp.arange(D_RWKV)[:, None] // RWKV_HEAD_DIM == jnp.arange(D_RWKV)[None, :] // RWKV_HEAD_DIM)
    head_avg = (head_avg.astype(F32) / RWKV_HEAD_DIM).astype(BF16)
    row_spec = lambda w: pl.BlockSpec((tm, w), lambda i: (i, 0))
    return pl.pallas_call(
        _out_proj_body,
        grid=(m // tm,),
        in_specs=[row_spec(d), row_spec(D_CONV), row_spec(D_NSA), row_spec(D_RWKV), row_spec(D_RWKV),
                  row_spec(D_RWKV), _const_spec((1, D_RWKV)), _const_spec((1, D_RWKV)),
                  _const_spec((D_RWKV, D_RWKV)), _const_spec(wc.shape), _const_spec(wn.shape),
                  _const_spec(wr.shape)],
        out_specs=row_spec(d),
        out_shape=jax.ShapeDtypeStruct((m, d), F32),
        compiler_params=_cparams("parallel"),
        name="out_proj",
    )(x2d, y_conv, y_nsa, y_scan, bv, gate, ln_g.reshape(1, -1), ln_b.reshape(1, -1), head_avg, wc, wn, wr)


def _ffn_body(x_ref, xh_ref, g_ref, wg_ref, wu_ref, cw_ref, cb_ref, wd_ref, o_ref, *, tiles_per_seq):
    tm = x_ref.shape[0]
    x = x_ref[...]
    hb = _rms(x, g_ref[...]).astype(BF16)
    hh = _rms(xh_ref[...], g_ref[...]).astype(BF16)
    a = _dot(hb, wg_ref[...])
    first = (pl.program_id(0) % tiles_per_seq) == 0
    ah = jnp.where(first, 0.0, _dot(hh, wg_ref[...]))
    row = lax.broadcasted_iota(jnp.int32, (tm, 1), 0)
    a1 = jnp.where(row == 0, ah[SUBLANES - 1:SUBLANES], pltpu.roll(a, 1, 0))
    a2 = jnp.where(row == 0, ah[SUBLANES - 2:SUBLANES - 1],
                   jnp.where(row == 1, ah[SUBLANES - 1:SUBLANES], pltpu.roll(a, 2, 0)))
    conv = cw_ref[0:1, :] * a2 + cw_ref[1:2, :] * a1 + cw_ref[2:3, :] * a + cb_ref[...]
    act = conv * _sigmoid(conv) * _dot(hb, wu_ref[...])
    o_ref[...] = x + _dot(act.astype(BF16), wd_ref[...])


def _ffn(x2d, seq, gain, wg, wu, conv_w, conv_b, wd, tm=512):
    m, d = x2d.shape
    f = wg.shape[1]
    hb = tm // SUBLANES
    cw = jnp.zeros((SUBLANES, f), F32).at[:FFN_CONV].set(conv_w)
    return pl.pallas_call(
        functools.partial(_ffn_body, tiles_per_seq=seq // tm),
        grid=(m // tm,),
        in_specs=[pl.BlockSpec((tm, d), lambda i: (i, 0)),
                  pl.BlockSpec((SUBLANES, d), lambda i: (jnp.maximum(i * hb - 1, 0), 0)),
                  _const_spec((1, d)), _const_spec(wg.shape), _const_spec(wu.shape),
                  _const_spec((SUBLANES, f)), _const_spec((1, f)), _const_spec(wd.shape)],
        out_specs=pl.BlockSpec((tm, d), lambda i: (i, 0)),
        out_shape=jax.ShapeDtypeStruct((m, d), F32),
        compiler_params=_cparams("parallel"),
        name="conv_ffn",
    )(x2d, x2d, gain.reshape(1, d), wg, wu, cw, conv_b.reshape(1, f), wd)


def _final_norm_body(x_ref, g_ref, o_ref):
    o_ref[...] = _rms(x_ref[...], g_ref[...])


def _final_norm(x2d, gain, tm=1024):
    m, d = x2d.shape
    return pl.pallas_call(
        _final_norm_body,
        grid=(m // tm,),
        in_specs=[pl.BlockSpec((tm, d), lambda i: (i, 0)), _const_spec((1, d))],
        out_specs=pl.BlockSpec((tm, d), lambda i: (i, 0)),
        out_shape=jax.ShapeDtypeStruct((m, d), F32),
        compiler_params=_cparams("parallel"),
        name="final_norm",
    )(x2d, gain.reshape(1, d))


def _split_w_in(w):
    o = 0
    wc = w[:, o:o + CONV_COLS]; o += CONV_COLS
    wq = w[:, o:o + D_NSA]; o += D_NSA
    wkv = w[:, o:o + 6 * KV_COLS]; o += 6 * KV_COLS
    wgt = w[:, o:o + 3 * NSA_HEADS]; o += 3 * NSA_HEADS
    wrkv = w[:, o:o + 3 * D_RWKV]; o += 3 * D_RWKV
    wlora = w[:, o:]
    per_group = 3 * GQ
    wgt_pad = jnp.zeros((w.shape[0], NSA_KV_HEADS * LANES), w.dtype)
    for g in range(NSA_KV_HEADS):
        wgt_pad = wgt_pad.at[:, g * LANES:g * LANES + per_group].set(wgt[:, g * per_group:(g + 1) * per_group])
    flat = [z.astype(BF16) for z in (wc, wq, wgt_pad, wrkv, wlora)]

    def per_group_cols(i, pad):
        wi = wkv[:, i * KV_COLS:(i + 1) * KV_COLS].reshape(w.shape[0], NSA_KV_HEADS, HEAD_DIM).transpose(1, 0, 2)
        if pad:
            wi = jnp.pad(wi, ((0, 0), (0, 0), (0, LANES - HEAD_DIM)))
        return wi.astype(BF16)

    kv = [per_group_cols(i, pad) for i, pad in enumerate((False, False, True, True, False, True))]
    return flat, kv


def _pad_lora(w, row0, n_rows):
    return jnp.zeros((n_rows, w.shape[1]), F32).at[row0:row0 + w.shape[0]].set(w).astype(BF16)


def kernel(x, w_in, w_out, norm_mix, norm_ffn, norm_final, conv_dw_w, conv_dw_b, conv_ln_g, conv_ln_b, cmp_pe_k, cmp_pe_v, cmp_w1_k, cmp_w2_k, cmp_w1_v, cmp_w2_v, rwkv_mu, rwkv_w0, rwkv_w2, rwkv_a0, rwkv_a2, rwkv_g2, rwkv_k_k, rwkv_k_a, rwkv_r_k, rwkv_ln_g, rwkv_ln_b, ffn_w_gate, ffn_w_up, ffn_conv_w, ffn_conv_b, ffn_w_down):
    b, t, d = x.shape
    m = b * t
    n_lora = DECAY_LORA + AAA_LORA + GATE_LORA
    x2 = x.reshape(m, d)
    for i in range(w_in.shape[0]):
        p_conv, q, gt, p_rkv, p_lora, *kv_ops = _norm_proj(x2, t, norm_mix[i], *_split_w_in(w_in[i]))
        y_conv = _conv_mixer(p_conv.reshape(b, t, -1), conv_dw_w[i], conv_dw_b[i], conv_ln_g[i], conv_ln_b[i])
        y_nsa = _nsa_mixer(q.reshape(b, t, -1), gt.reshape(b, t, -1), *kv_ops,
                           cmp_pe_k[i], cmp_pe_v[i], cmp_w1_k[i], cmp_w2_k[i], cmp_w1_v[i], cmp_w2_v[i])
        r, w, k2, nkk, kka, v, bv, gate = _rwkv_prep(
            p_rkv, p_lora, t, rwkv_mu[i], rwkv_w0[i],
            _pad_lora(rwkv_w2[i], 0, n_lora), rwkv_a0[i],
            _pad_lora(rwkv_a2[i], DECAY_LORA, n_lora),
            _pad_lora(rwkv_g2[i], DECAY_LORA + AAA_LORA, n_lora),
            rwkv_k_k[i], rwkv_k_a[i], rwkv_r_k[i].reshape(-1))
        s3 = lambda z: z.reshape(b, t, D_RWKV)
        y_scan = _rwkv_scan(s3(r), s3(w), s3(k2), s3(nkk), s3(kka), s3(v))
        wo = w_out[i].astype(BF16)
        x2 = _out_proj(x2, y_conv.reshape(m, -1), y_nsa.reshape(m, -1), y_scan.reshape(m, -1), bv, gate,
                       rwkv_ln_g[i], rwkv_ln_b[i], wo[:D_CONV], wo[D_CONV:D_CONV + D_NSA], wo[D_CONV + D_NSA:])
        x2 = _ffn(x2, t, norm_ffn[i], ffn_w_gate[i].astype(BF16), ffn_w_up[i].astype(BF16),
                  ffn_conv_w[i], ffn_conv_b[i], ffn_w_down[i].astype(BF16))
    return _final_norm(x2, norm_final).reshape(b, t, d)
```
